```python
import math
import jax, jax.numpy as jnp
from jax import lax
import numpy as np

D_MODEL = 4096
BATCH = 8
SEQ = 2048
DEPTH = 2

PLE_DIM = 256
HEAD_DIM = 128
MOBA_HEADS = 8
MOBA_BLOCK = 256
MOBA_TOPK = 3
MOBA_QCHUNK = 16
GDN_HEADS = 16
GDN_DK = 128
GDN_DV = 128
GDN_CONV = 4
GDN_CHUNK = 64
MLA_HEADS = 8
MLA_Q_RANK = 768
MLA_KV_RANK = 512
MLA_NOPE = 128
MLA_ROPE = 64
MLA_V = 128
MLA_QBLOCK = 128
ROPE_THETA = 10000.0
D_FF = -(-8 * D_MODEL // (3 * 256)) * 256

MOBA_W = MOBA_HEADS * HEAD_DIM
GDN_KW = GDN_HEADS * GDN_DK
GDN_VW = GDN_HEADS * GDN_DV
MLA_VW = MLA_HEADS * MLA_V
IN_SIZES = (MOBA_W, MOBA_W, MOBA_W, GDN_KW, GDN_KW, GDN_VW, GDN_HEADS, GDN_HEADS, GDN_VW,
            MLA_Q_RANK, MLA_KV_RANK, MLA_ROPE)
IN_WIDTH = sum(IN_SIZES)

kernel_name = 'hybrid_moba_gdn_mla_gated_block'


def _rmsnorm(x, gain, eps=1e-6):
    xf = x.astype(jnp.float32)
    xf = xf * lax.rsqrt(jnp.mean(xf * xf, axis=-1, keepdims=True) + eps)
    return (xf * gain.astype(jnp.float32)).astype(x.dtype)


def _l2norm(x, eps=1e-6):
    return x * lax.rsqrt(jnp.sum(x * x, axis=-1, keepdims=True) + eps)


def _alibi_slopes(n_heads):
    return 2.0 ** (-8.0 * jnp.arange(1, n_heads + 1, dtype=jnp.float32) / n_heads)


def _rope(x, pos):
    half = x.shape[-1] // 2
    inv_freq = 1.0 / (ROPE_THETA ** (jnp.arange(half, dtype=jnp.float32) / half))
    ang = pos.astype(jnp.float32)[..., None] * inv_freq
    ang = ang.reshape(ang.shape[:2] + (1,) * (x.ndim - 3) + (half,))
    cos, sin = jnp.cos(ang), jnp.sin(ang)
    xf = x.astype(jnp.float32)
    x1, x2 = xf[..., :half], xf[..., half:]
    return jnp.concatenate([x1 * cos - x2 * sin, x1 * sin + x2 * cos], axis=-1).astype(x.dtype)


def _moba_attention(q, k, v, pos):
    B, S, H, dh = q.shape
    blk, qcl = MOBA_BLOCK, MOBA_QCHUNK
    nb = -(-S // blk)
    sp = nb * blk
    pad = ((0, 0), (0, sp - S), (0, 0), (0, 0))
    q, k, v = (jnp.pad(t, pad).transpose(0, 2, 1, 3) for t in (q, k, v))
    posp = jnp.pad(pos, ((0, 0), (0, sp - S)), mode='edge')
    kb = k.reshape(B, H, nb, blk, dh)
    vb = v.reshape(B, H, nb, blk, dh)
    pb = posp.reshape(B, nb, blk)
    kmean = jnp.mean(kb.astype(jnp.float32), axis=3)
    gate = jnp.einsum('bhsd,bhnd->bhsn', q.astype(jnp.float32), kmean)
    q_block = jnp.arange(sp) // blk
    fully_past = jnp.arange(nb)[None, :] < q_block[:, None]
    gate = jnp.where(fully_past, gate, -jnp.inf)
    n_sel = min(MOBA_TOPK, nb)
    _, sel = lax.top_k(gate, n_sel)
    slopes = _alibi_slopes(H)
    scale = dh ** -0.5
    bi = jnp.arange(B)[:, None, None, None]
    hi = jnp.arange(H)[None, :, None, None]
    nc = sp // qcl
    q_chunks = q.reshape(B, H, nc, qcl, dh).transpose(2, 0, 1, 3, 4)
    sel_chunks = sel.reshape(B, H, nc, qcl, n_sel).transpose(2, 0, 1, 3, 4)

    def chunk(args):
        qc, ic, c = args
        start = c * qcl
        own = start // blk
        tq = start + jnp.arange(qcl)
        pq = lax.dynamic_slice_in_dim(posp, start, qcl, axis=1)
        ks = kb[bi, hi, ic]
        vs = vb[bi, hi, ic]
        ps = pb[bi, ic]
        s_sel = jnp.einsum('bhqd,bhqnkd->bhqnk', qc, ks).astype(jnp.float32) * scale
        s_sel = s_sel - slopes[:, None, None, None] * jnp.abs(pq[:, None, :, None, None] - ps).astype(jnp.float32)
        s_sel = jnp.where((ic < own)[..., None], s_sel, -jnp.inf).reshape(B, H, qcl, n_sel * blk)
        ko = lax.dynamic_slice_in_dim(k, own * blk, blk, axis=2)
        vo = lax.dynamic_slice_in_dim(v, own * blk, blk, axis=2)
        po = lax.dynamic_slice_in_dim(posp, own * blk, blk, axis=1)
        s_own = jnp.einsum('bhqd,bhkd->bhqk', qc, ko).astype(jnp.float32) * scale
        s_own = s_own - slopes[:, None, None] * jnp.abs(pq[:, None, :, None] - po[:, None, None, :]).astype(jnp.float32)
        tk = own * blk + jnp.arange(blk)
        s_own = jnp.where(tk[None, :] <= tq[:, None], s_own, -jnp.inf)
        probs = jax.nn.softmax(jnp.concatenate([s_sel, s_own], axis=-1), axis=-1).astype(v.dtype)
        o = jnp.einsum('bhqk,bhqkd->bhqd', probs[..., :n_sel * blk], vs.reshape(B, H, qcl, n_sel * blk, dh))
        return o + jnp.einsum('bhqk,bhkd->bhqd', probs[..., n_sel * blk:], vo)

    out = lax.map(chunk, (q_chunks, sel_chunks, jnp.arange(nc)))
    return out.transpose(1, 0, 3, 2, 4).reshape(B, sp, H * dh)[:, :S]


def _gated_delta_net(q, k, v, a, b, z, conv_w, a_log, dt_bias, norm_w):
    B, S, _ = q.shape
    H, dk, dv, L = GDN_HEADS, GDN_DK, GDN_DV, GDN_CHUNK
    qkv = jnp.concatenate([q, k, v], axis=-1)
    ch = qkv.shape[-1]
    qkv = lax.conv_general_dilated(qkv, conv_w[:, None, :].astype(qkv.dtype), window_strides=(1,),
                                   padding=[(GDN_CONV - 1, 0)], dimension_numbers=('NWC', 'WIO', 'NWC'),
                                   feature_group_count=ch)
    qkv = jax.nn.silu(qkv.astype(jnp.float32))
    q, k, v = jnp.split(qkv, [H * dk, 2 * H * dk], axis=-1)
    q = _l2norm(q.reshape(B, S, H, dk)) * dk ** -0.5
    k = _l2norm(k.reshape(B, S, H, dk))
    v = v.reshape(B, S, H, dv)
    beta = jax.nn.sigmoid(b.astype(jnp.float32))
    g = -jnp.exp(a_log.astype(jnp.float32)) * jax.nn.softplus(a.astype(jnp.float32) + dt_bias.astype(jnp.float32))
    n = S // L

    def chunks(t):
        return t.reshape(B, n, L, H, -1).transpose(0, 3, 1, 2, 4)

    qc, kc, vc = chunks(q), chunks(k), chunks(v)
    gc = jnp.cumsum(chunks(g[..., None])[..., 0], axis=-1)
    bc = chunks(beta[..., None])
    incl = jnp.tril(jnp.ones((L, L), dtype=bool))
    strict = jnp.tril(jnp.ones((L, L), dtype=bool), -1)
    decay = jnp.exp(jnp.where(incl, gc[..., :, None] - gc[..., None, :], -jnp.inf))
    kbeta = kc * bc
    a_kk = jnp.where(strict, jnp.einsum('bhnid,bhnjd->bhnij', kbeta, kc) * decay, 0.0)
    eye = jnp.eye(L, dtype=jnp.float32)
    rhs = jnp.concatenate([vc * bc, kbeta * jnp.exp(gc)[..., None]], axis=-1)
    uw = lax.linalg.triangular_solve(a_kk + eye, rhs, left_side=True, lower=True, unit_diagonal=True)
    u, w = uw[..., :dv], uw[..., dv:]
    a_qk = jnp.where(incl, jnp.einsum('bhnid,bhnjd->bhnij', qc, kc) * decay, 0.0)
    q_dec = qc * jnp.exp(gc)[..., None]
    k_tail = kc * jnp.exp(gc[..., -1:] - gc)[..., None]
    g_last = jnp.exp(gc[..., -1])

    def step(state, xs):
        qd, kt, uu, ww, aqk, gl = xs
        v_new = uu - jnp.einsum('bhlk,bhkv->bhlv', ww, state)
        o = jnp.einsum('bhlk,bhkv->bhlv', qd, state) + jnp.einsum('bhij,bhjv->bhiv', aqk, v_new)
        state = state * gl[..., None, None] + jnp.einsum('bhlk,bhlv->bhkv', kt, v_new)
        return state, o

    def mv(t):
        return jnp.moveaxis(t, 2, 0)

    s0 = jnp.zeros((B, H, dk, dv), jnp.float32)
    _, o = lax.scan(step, s0, (mv(q_dec), mv(k_tail), mv(u), mv(w), mv(a_qk), mv(g_last)))
    o = o.transpose(1, 0, 3, 2, 4).reshape(B, S, H, dv)
    o = _rmsnorm(o, norm_w) * jax.nn.silu(z.reshape(B, S, H, dv).astype(jnp.float32))
    return o.reshape(B, S, H * dv).astype(z.dtype)


def _causal_attention(q, k, v, scale):
    B, S, H, dq = q.shape
    dv = v.shape[-1]
    qb = MLA_QBLOCK
    nq = S // qb
    q_blocks = q.reshape(B, nq, qb, H, dq).transpose(1, 0, 2, 3, 4)
    tk = jnp.arange(S)

    def block(args):
        qi, i = args
        s = jnp.einsum('bqhd,bkhd->bhqk', qi, k).astype(jnp.float32) * scale
        tq = i * qb + jnp.arange(qb)
        s = jnp.where(tk[None, :] <= tq[:, None], s, -jnp.inf)
        probs = jax.nn.softmax(s, axis=-1).astype(v.dtype)
        return jnp.einsum('bhqk,bkhd->bqhd', probs, v)

    o = lax.map(block, (q_blocks, jnp.arange(nq)))
    return o.transpose(1, 0, 2, 3, 4).reshape(B, S, H * dv)


def _mla_attention(c_q, c_kv, k_rope, pos, q_norm_w, w_uq, kv_norm_w, w_ukv):
    B, S, _ = c_q.shape
    H = MLA_HEADS
    q = (_rmsnorm(c_q, q_norm_w) @ w_uq).reshape(B, S, H, MLA_NOPE + MLA_ROPE)
    kv = (_rmsnorm(c_kv, kv_norm_w) @ w_ukv).reshape(B, S, H, MLA_NOPE + MLA_V)
    q_nope, q_pe = q[..., :MLA_NOPE], q[..., MLA_NOPE:]
    k_nope, v = kv[..., :MLA_NOPE], kv[..., MLA_NOPE:]
    q_pe = _rope(q_pe, pos)
    k_pe = _rope(k_rope, pos)
    qf = jnp.concatenate([q_nope, q_pe], axis=-1)
    kf = jnp.concatenate([k_nope, jnp.broadcast_to(k_pe[:, :, None, :], (B, S, H, MLA_ROPE))], axis=-1)
    return _causal_attention(qf, kf, v, (MLA_NOPE + MLA_ROPE) ** -0.5)


def setup_inputs(seed: int = 0) -> dict:
    key = jax.random.key(seed)
    ks = jax.random.split(key, 32)
    Lr, D = DEPTH, D_MODEL

    def dense(k, shape, fan_in):
        return jax.random.normal(k, shape, jnp.float32) * fan_in ** -0.5

    def gain(k, shape):
        return 1.0 + 0.05 * jax.random.normal(k, shape, jnp.float32)

    x = jax.random.normal(ks[0], (BATCH, SEQ, D), jnp.float32)
    p = jax.random.normal(ks[1], (Lr, BATCH, SEQ, PLE_DIM), jnp.float32)
    start = jax.random.randint(ks[2], (BATCH, 1), 0, 1024, dtype=jnp.int32)
    steps = jax.random.randint(ks[3], (BATCH, SEQ), 1, 3, dtype=jnp.int32)
    positions = start + jnp.cumsum(steps, axis=1) - 1
    dt = jnp.exp(jax.random.uniform(ks[4], (Lr, GDN_HEADS), jnp.float32, math.log(1e-3), math.log(0.1)))
    gdn_dt_bias = dt + jnp.log(-jnp.expm1(-dt))
    gdn_a_log = jnp.log(jax.random.uniform(ks[5], (Lr, GDN_HEADS), jnp.float32, 1.0, 16.0))
    return {
        'x': x,
        'p': p,
        'positions': positions,
        'norm_mix_in': gain(ks[6], (Lr, D)),
        'w_in': dense(ks[7], (Lr, D, IN_WIDTH), D),
        'gdn_conv_w': dense(ks[8], (Lr, GDN_CONV, 2 * GDN_KW + GDN_VW), GDN_CONV),
        'gdn_a_log': gdn_a_log,
        'gdn_dt_bias': gdn_dt_bias,
        'gdn_norm_w': gain(ks[9], (Lr, GDN_DV)),
        'mla_q_norm_w': gain(ks[10], (Lr, MLA_Q_RANK)),
        'mla_w_uq': dense(ks[11], (Lr, MLA_Q_RANK, MLA_HEADS * (MLA_NOPE + MLA_ROPE)), MLA_Q_RANK),
        'mla_kv_norm_w': gain(ks[12], (Lr, MLA_KV_RANK)),
        'mla_w_ukv': dense(ks[13], (Lr, MLA_KV_RANK, MLA_HEADS * (MLA_NOPE + MLA_V)), MLA_KV_RANK),
        'w_branch_gate': dense(ks[14], (Lr, D, 3 * D), D),
        'w_branch_a': dense(ks[15], (Lr, MOBA_W, D), MOBA_W),
        'w_branch_b': dense(ks[16], (Lr, GDN_VW, D), GDN_VW),
        'w_branch_c': dense(ks[17], (Lr, MLA_VW, D), MLA_VW),
        'w_out': dense(ks[18], (Lr, D, D), D),
        'norm_mix_out': gain(ks[19], (Lr, D)),
        'norm_ffn_in': gain(ks[20], (Lr, D)),
        'w_ffn_gate': dense(ks[21], (Lr, D, D_FF), D),
        'w_ffn_up': dense(ks[22], (Lr, D, D_FF), D),
        'w_ffn_down': dense(ks[23], (Lr, D_FF, D), D_FF),
        'norm_ffn_out': gain(ks[24], (Lr, D)),
        'w_ple_gate': dense(ks[25], (Lr, D, D), D),
        'w_ple_proj': dense(ks[26], (Lr, PLE_DIM, D), PLE_DIM),
    }


def reference(x, p, positions, norm_mix_in, w_in, gdn_conv_w, gdn_a_log, gdn_dt_bias, gdn_norm_w,
              mla_q_norm_w, mla_w_uq, mla_kv_norm_w, mla_w_ukv, w_branch_gate, w_branch_a, w_branch_b,
              w_branch_c, w_out, norm_mix_out, norm_ffn_in, w_ffn_gate, w_ffn_up, w_ffn_down,
              norm_ffn_out, w_ple_gate, w_ple_proj):
    B, S, D = x.shape
    split_at = np.cumsum(IN_SIZES)[:-1].tolist()
    moba_shape = (B, S, MOBA_HEADS, HEAD_DIM)
    for i in range(DEPTH):
        h = _rmsnorm(x, norm_mix_in[i])
        (a_q, a_k, a_v, b_q, b_k, b_v, b_a, b_b, b_z, c_q, c_kv, c_kr) = jnp.split(h @ w_in[i], split_at, axis=-1)
        y_a = _moba_attention(a_q.reshape(moba_shape), a_k.reshape(moba_shape), a_v.reshape(moba_shape), positions)
        y_b = _gated_delta_net(b_q, b_k, b_v, b_a, b_b, b_z, gdn_conv_w[i], gdn_a_log[i], gdn_dt_bias[i], gdn_norm_w[i])
        y_c = _mla_attention(c_q, c_kv, c_kr, positions, mla_q_norm_w[i], mla_w_uq[i], mla_kv_norm_w[i], mla_w_ukv[i])
        g_a, g_b, g_c = jnp.split(jax.nn.sigmoid(h @ w_branch_gate[i]), 3, axis=-1)
        merged = g_a * (y_a @ w_branch_a[i]) + g_b * (y_b @ w_branch_b[i]) + g_c * (y_c @ w_branch_c[i])
        x = x + _rmsnorm(merged @ w_out[i], norm_mix_out[i])
        h = _rmsnorm(x, norm_ffn_in[i])
        f = (jax.nn.silu(h @ w_ffn_gate[i]) * (h @ w_ffn_up[i])) @ w_ffn_down[i]
        x = x + _rmsnorm(f, norm_ffn_out[i])
        x = x + jax.nn.sigmoid(x @ w_ple_gate[i]) * (p[i] @ w_ple_proj[i])
    return x
```

```python
import functools

import numpy as np
import jax
import jax.numpy as jnp
from jax import lax
from jax.experimental import pallas as pl
from jax.experimental.pallas import tpu as pltpu

F32 = jnp.float32
BF16 = jnp.bfloat16

HEAD_DIM = 128
MOBA_HEADS = 8
MOBA_BLOCK = 256
MOBA_TOPK = 3
GDN_HEADS = 16
GDN_DK = 128
GDN_DV = 128
GDN_CONV = 4
GDN_CHUNK = 64
MLA_HEADS = 8
MLA_Q_RANK = 768
MLA_KV_RANK = 512
MLA_NOPE = 128
MLA_ROPE = 64
MLA_V = 128
ROPE_THETA = 10000.0
NORM_EPS = 1e-6

LANES = 128
VMEM_LIMIT_BYTES = 52 * 2 ** 20

GDN_HEADS_PER_STEP = 4
GDN_SEQ_TILE = 512
ATTN_TILE = 256


def _cparams(*sem):
    return pltpu.CompilerParams(dimension_semantics=sem, vmem_limit_bytes=VMEM_LIMIT_BYTES)


def _dot(a, b):
    return jnp.dot(a, b, preferred_element_type=F32)


def _dot_nt(a, b):
    return lax.dot_general(a, b, (((1,), (1,)), ((), ())), preferred_element_type=F32)


def _dot_tn(a, b):
    return lax.dot_general(a, b, (((0,), (0,)), ((), ())), preferred_element_type=F32)


def _sigmoid(x):
    return jax.nn.sigmoid(x)


def _rms(x, gain):
    return x * lax.rsqrt(jnp.mean(x * x, axis=-1, keepdims=True) + NORM_EPS) * gain


def _rms_kernel(x_ref, g_ref, o_ref):
    o_ref[...] = _rms(x_ref[...], g_ref[...]).astype(o_ref.dtype)


def _rmsnorm(x, gain, tm=256):
    m, d = x.shape
    return pl.pallas_call(
        _rms_kernel,
        grid=(m // tm,),
        in_specs=[pl.BlockSpec((tm, d), lambda i: (i, 0)), pl.BlockSpec((1, d), lambda i: (0, 0))],
        out_specs=pl.BlockSpec((tm, d), lambda i: (i, 0)),
        out_shape=jax.ShapeDtypeStruct((m, d), BF16),
        compiler_params=_cparams("parallel"),
        name="rmsnorm",
    )(x, gain.reshape(1, d))


def _res_rms_kernel(x_ref, y_ref, g_ref, g2_ref, xo_ref, ho_ref, *, norm_next):
    xn = x_ref[...] + _rms(y_ref[...], g_ref[...])
    xo_ref[...] = xn
    if norm_next:
        ho_ref[...] = _rms(xn, g2_ref[...]).astype(ho_ref.dtype)
    else:
        ho_ref[...] = xn.astype(ho_ref.dtype)


def _residual_rmsnorm(x, y, gain, gain_next, norm_next, tm=256):
    m, d = x.shape
    row = pl.BlockSpec((tm, d), lambda i: (i, 0))
    vec = pl.BlockSpec((1, d), lambda i: (0, 0))
    return pl.pallas_call(
        functools.partial(_res_rms_kernel, norm_next=norm_next),
        grid=(m // tm,),
        in_specs=[row, row, vec, vec],
        out_specs=[row, row],
        out_shape=[jax.ShapeDtypeStruct((m, d), F32), jax.ShapeDtypeStruct((m, d), BF16)],
        compiler_params=_cparams("parallel"),
        name="residual_rmsnorm",
    )(x, y, gain.reshape(1, d), gain_next.reshape(1, d))


def _mm_kernel(a_ref, w_ref, o_ref):
    o_ref[...] = _dot(a_ref[...], w_ref[...]).astype(o_ref.dtype)


def _matmul(a, w, out_dtype, tm, tn, name):
    m, k = a.shape
    n = w.shape[1]
    return pl.pallas_call(
        _mm_kernel,
        grid=(m // tm, n // tn),
        in_specs=[pl.BlockSpec((tm, k), lambda i, j: (i, 0)), pl.BlockSpec((k, tn), lambda i, j: (0, j))],
        out_specs=pl.BlockSpec((tm, tn), lambda i, j: (i, j)),
        out_shape=jax.ShapeDtypeStruct((m, n), out_dtype),
        compiler_params=_cparams("parallel", "arbitrary"),
        name=name,
    )(a, w)


def _mm_acc_kernel(a_ref, w_ref, o_ref, acc_ref):
    kk = pl.program_id(2)

    @pl.when(kk == 0)
    def _():
        acc_ref[...] = jnp.zeros_like(acc_ref)

    acc_ref[...] += _dot(a_ref[...], w_ref[...])

    @pl.when(kk == pl.num_programs(2) - 1)
    def _():
        o_ref[...] = acc_ref[...].astype(o_ref.dtype)


def _matmul_ksplit(a, w, out_dtype, tm, tn, tk, name):
    m, k = a.shape
    n = w.shape[1]
    return pl.pallas_call(
        _mm_acc_kernel,
        grid=(m // tm, n // tn, k // tk),
        in_specs=[pl.BlockSpec((tm, tk), lambda i, j, kk: (i, kk)), pl.BlockSpec((tk, tn), lambda i, j, kk: (kk, j))],
        out_specs=pl.BlockSpec((tm, tn), lambda i, j, kk: (i, j)),
        out_shape=jax.ShapeDtypeStruct((m, n), out_dtype),
        scratch_shapes=[pltpu.VMEM((tm, tn), F32)],
        compiler_params=_cparams("parallel", "parallel", "arbitrary"),
        name=name,
    )(a, w)


def _merge_kernel(h_ref, wga_ref, wgb_ref, wgc_ref, ya_ref, yb_ref, yc_ref, wa_ref, wb_ref, wc_ref, o_ref):
    h = h_ref[...]
    out = _sigmoid(_dot(h, wga_ref[...])) * _dot(ya_ref[...], wa_ref[...])
    out += _sigmoid(_dot(h, wgb_ref[...])) * _dot(yb_ref[...], wb_ref[...])
    out += _sigmoid(_dot(h, wgc_ref[...])) * _dot(yc_ref[...], wc_ref[...])
    o_ref[...] = out.astype(o_ref.dtype)


def _gated_merge(h, w_gate, ya, yb, yc, wa, wb, wc, tm=512, tn=256):
    m, d = h.shape
    nj = d // tn
    res = lambda width: pl.BlockSpec((tm, width), lambda i, j: (i, 0))
    gate = lambda br: pl.BlockSpec((d, tn), lambda i, j: (0, br * nj + j))
    wcol = lambda width: pl.BlockSpec((width, tn), lambda i, j: (0, j))
    return pl.pallas_call(
        _merge_kernel,
        grid=(m // tm, nj),
        in_specs=[res(d), gate(0), gate(1), gate(2), res(ya.shape[1]), res(yb.shape[1]), res(yc.shape[1]),
                  wcol(wa.shape[0]), wcol(wb.shape[0]), wcol(wc.shape[0])],
        out_specs=pl.BlockSpec((tm, tn), lambda i, j: (i, j)),
        out_shape=jax.ShapeDtypeStruct((m, d), BF16),
        compiler_params=_cparams("parallel", "arbitrary"),
        name="gated_merge",
    )(h, w_gate, w_gate, w_gate, ya, yb, yc, wa, wb, wc)


def _ffn_up_kernel(h_ref, wg_ref, wu_ref, o_ref):
    h = h_ref[...]
    g = _dot(h, wg_ref[...])
    o_ref[...] = (g * _sigmoid(g) * _dot(h, wu_ref[...])).astype(o_ref.dtype)


def _ffn_up(h, wg, wu, tm=1024, tn=256):
    m, d = h.shape
    n = wg.shape[1]
    wspec = pl.BlockSpec((d, tn), lambda i, j: (0, j))
    return pl.pallas_call(
        _ffn_up_kernel,
        grid=(m // tm, n // tn),
        in_specs=[pl.BlockSpec((tm, d), lambda i, j: (i, 0)), wspec, wspec],
        out_specs=pl.BlockSpec((tm, tn), lambda i, j: (i, j)),
        out_shape=jax.ShapeDtypeStruct((m, n), BF16),
        compiler_params=_cparams("parallel", "arbitrary"),
        name="ffn_up",
    )(h, wg, wu)


def _ple_kernel(xb_ref, x_ref, p_ref, wg_ref, wp_ref, o_ref):
    gate = _sigmoid(_dot(xb_ref[...], wg_ref[...]))
    o_ref[...] = x_ref[...] + gate * _dot(p_ref[...], wp_ref[...])


def _ple(xb, x, p, wg, wp, tm=1024, tn=512):
    m, d = x.shape
    return pl.pallas_call(
        _ple_kernel,
        grid=(m // tm, d // tn),
        in_specs=[pl.BlockSpec((tm, d), lambda i, j: (i, 0)),
                  pl.BlockSpec((tm, tn), lambda i, j: (i, j)),
                  pl.BlockSpec((tm, p.shape[1]), lambda i, j: (i, 0)),
                  pl.BlockSpec((d, tn), lambda i, j: (0, j)),
                  pl.BlockSpec((p.shape[1], tn), lambda i, j: (0, j))],
        out_specs=pl.BlockSpec((tm, tn), lambda i, j: (i, j)),
        out_shape=jax.ShapeDtypeStruct((m, d), F32),
        compiler_params=_cparams("parallel", "arbitrary"),
        name="ple",
    )(xb, x, p, wg, wp)


def _softmax_first(s, v, m_ref, l_ref, acc_ref):
    m = jnp.max(s, axis=-1, keepdims=True)
    p = jnp.exp(s - m)
    m_ref[...] = m
    l_ref[...] = jnp.sum(p, axis=-1, keepdims=True)
    acc_ref[...] = _dot(p.astype(BF16), v)


def _softmax_update(s, v, m_ref, l_ref, acc_ref):
    m_old = m_ref[...]
    m_new = jnp.maximum(m_old, jnp.max(s, axis=-1, keepdims=True))
    alpha = jnp.exp(m_old - m_new)
    p = jnp.exp(s - m_new)
    m_ref[...] = m_new
    l_ref[...] = alpha * l_ref[...] + jnp.sum(p, axis=-1, keepdims=True)
    acc_ref[...] = alpha * acc_ref[...] + _dot(p.astype(BF16), v)


def _moba_kernel(slope_ref, q_ref, k_ref, v_ref, pq_ref, pko_ref, pk_ref, o_ref,
                 km_ref, m_ref, l_ref, acc_ref, *, blk, nb, topk, scale):
    h = pl.program_id(1)
    qb = pl.program_id(2)
    slope = slope_ref[h]

    @pl.when(qb == 0)
    def _():
        km_ref[...] = jnp.zeros_like(km_ref)
        for j in range(nb):
            km_ref[j:j + 1, :] = jnp.mean(k_ref[j * blk:(j + 1) * blk, :].astype(F32), axis=0, keepdims=True)

    q = q_ref[...]
    pq = pq_ref[0]

    km = km_ref[...]
    km_hi = km.astype(BF16)
    km_lo = (km - km_hi.astype(F32)).astype(BF16)
    gate = _dot_nt(q, km_hi) + _dot_nt(q, km_lo)
    lane = lax.broadcasted_iota(jnp.int32, gate.shape, 1)
    past = lane < qb
    selected = []
    for j in range(nb - 1):
        gj = gate[:, j:j + 1]
        beats = jnp.where(gate > gj, 1.0, jnp.where(gate == gj, jnp.where(lane < j, 1.0, 0.0), 0.0))
        rank = jnp.sum(jnp.where(past, beats, 0.0), axis=-1, keepdims=True)
        selected.append(rank < topk)

    def scores(kj, pk):
        s = _dot_nt(q, kj) * scale
        return s - slope * jnp.abs(pq - pk).astype(F32)

    r0 = pl.multiple_of(qb * blk, blk)
    s = scores(k_ref[pl.ds(r0, blk), :], pko_ref[0])
    row = lax.broadcasted_iota(jnp.int32, s.shape, 0)
    col = lax.broadcasted_iota(jnp.int32, s.shape, 1)
    s = jnp.where(col <= row, s, -jnp.inf)
    _softmax_first(s, v_ref[pl.ds(r0, blk), :], m_ref, l_ref, acc_ref)

    for j in range(nb - 1):
        @pl.when(j < qb)
        def _(j=j):
            sl = slice(j * blk, (j + 1) * blk)
            sj = scores(k_ref[sl, :], pk_ref[0][:, sl])
            sj = jnp.where(selected[j], sj, -jnp.inf)
            _softmax_update(sj, v_ref[sl, :], m_ref, l_ref, acc_ref)

    o_ref[...] = (acc_ref[...] / l_ref[...]).astype(o_ref.dtype)


def _moba(qkv, pos_col, pos_row, slopes, batch, seq):
    blk, nh, dh = MOBA_BLOCK, MOBA_HEADS, HEAD_DIM
    assert seq % blk == 0
    nb = seq // blk
    kern = functools.partial(_moba_kernel, blk=blk, nb=nb, topk=min(MOBA_TOPK, nb), scale=dh ** -0.5)
    return pl.pallas_call(
        kern,
        grid=(batch, nh, nb),
        in_specs=[pl.BlockSpec(memory_space=pltpu.SMEM),
                  pl.BlockSpec((blk, dh), lambda b, h, i: (b * nb + i, h)),
                  pl.BlockSpec((seq, dh), lambda b, h, i: (b, nh + h)),
                  pl.BlockSpec((seq, dh), lambda b, h, i: (b, 2 * nh + h)),
                  pl.BlockSpec((1, blk, 1), lambda b, h, i: (b, i, 0)),
                  pl.BlockSpec((1, 1, blk), lambda b, h, i: (b, 0, i)),
                  pl.BlockSpec((1, 1, seq), lambda b, h, i: (b, 0, 0))],
        out_specs=pl.BlockSpec((blk, dh), lambda b, h, i: (b * nb + i, h)),
        out_shape=jax.ShapeDtypeStruct((batch * seq, nh * dh), BF16),
        scratch_shapes=[pltpu.VMEM((LANES, dh), F32), pltpu.VMEM((blk, 1), F32),
                        pltpu.VMEM((blk, 1), F32), pltpu.VMEM((blk, dh), F32)],
        compiler_params=_cparams("parallel", "parallel", "arbitrary"),
        name="moba_attention",
    )(slopes, qkv, qkv, qkv, pos_col, pos_row, pos_row)


def _gdn_prep_kernel(x_ref, w_ref, o_ref, xs_ref, *, seq, heads, slab, qscale):
    c = pl.program_id(1)
    pad = 8
    xs_ref[0:pad, :] = jnp.zeros((pad, LANES), F32)
    xs_ref[pad:pad + seq, :] = x_ref[...].astype(F32)
    w = w_ref[...]
    taps = w.shape[0]
    for r in range(0, seq, slab):
        y = jnp.zeros((slab, LANES), F32)
        for t in range(taps):
            start = pad + r - (taps - 1 - t)
            y = y + w[t:t + 1, :] * xs_ref[start:start + slab, :]
        y = y * _sigmoid(y)
        rs = lax.rsqrt(jnp.sum(y * y, axis=-1, keepdims=True) + NORM_EPS)
        fac = jnp.where(c < heads, rs * qscale, jnp.where(c < 2 * heads, rs, 1.0))
        o_ref[r:r + slab, :] = (y * fac).astype(o_ref.dtype)


def _gdn_prep(big, conv_w, col0, batch, seq):
    nh, dk = GDN_HEADS, GDN_DK
    ncol = conv_w.shape[1] // LANES
    kern = functools.partial(_gdn_prep_kernel, seq=seq, heads=nh, slab=256, qscale=dk ** -0.5)
    return pl.pallas_call(
        kern,
        grid=(batch, ncol),
        in_specs=[pl.BlockSpec((seq, LANES), lambda b, c: (b, col0 + c)),
                  pl.BlockSpec((conv_w.shape[0], LANES), lambda b, c: (0, c))],
        out_specs=pl.BlockSpec((seq, LANES), lambda b, c: (b, c)),
        out_shape=jax.ShapeDtypeStruct((batch * seq, conv_w.shape[1]), BF16),
        scratch_shapes=[pltpu.VMEM((seq + 8, LANES), F32)],
        compiler_params=_cparams("parallel", "parallel"),
        name="gdn_conv_norm",
    )(big, conv_w)


def _split3(x):
    p1 = x.astype(BF16)
    r1 = x - p1.astype(F32)
    p2 = r1.astype(BF16)
    p3 = (r1 - p2.astype(F32)).astype(BF16)
    return p1, p2, p3


def _unit_lower_inverse(a, row, col):
    n = a.shape[0]
    lower = col < row
    same8 = (row >> 3) == (col >> 3)
    p = jnp.where(same8, -a, 0.0)
    t = jnp.where(row == col, 1.0, 0.0) + p
    pb = p.astype(BF16)
    p2 = _dot(pb, pb)
    t = t + _dot(t.astype(BF16), p2.astype(BF16))
    p2b = p2.astype(BF16)
    p4 = _dot(p2b, p2b)
    t = t + _dot(t.astype(BF16), p4.astype(BF16))
    shift = 4
    while (1 << (shift - 1)) < n:
        off = ((row >> shift) == (col >> shift)) & ((row >> (shift - 1)) != (col >> (shift - 1))) & lower
        mb = jnp.where(off, a, 0.0).astype(BF16)
        tb = t.astype(BF16)
        t = t - _dot(tb, _dot(mb, tb).astype(BF16))
        shift += 1
    return t


def _gdn_kernel(q_ref, k_ref, v_ref, z_ref, gt_ref, alog_ref, dtb_ref, nw_ref, o_ref, state_ref, *, hb, ts, chunk):
    L = chunk

    @pl.when(pl.program_id(2) == 0)
    def _():
        state_ref[...] = jnp.zeros_like(state_ref)

    row = lax.broadcasted_iota(jnp.int32, (L, L), 0)
    col = lax.broadcasted_iota(jnp.int32, (L, L), 1)
    incl = col <= row
    strict = col < row
    tril = jnp.where(incl, 1.0, 0.0).astype(BF16)
    lane = lax.broadcasted_iota(jnp.int32, (L, LANES), 1)
    alog = alog_ref[0]
    dtb = dtb_ref[0]
    nw = nw_ref[...]

    def chunk_step(c, carry):
        r0 = pl.multiple_of(c * L, L)
        rows = pl.ds(r0, L)
        gl = gt_ref[rows, :]
        xg = gl + dtb
        softplus = jnp.maximum(xg, 0.0) + jnp.log1p(jnp.exp(-jnp.abs(xg)))
        g_all = -jnp.exp(alog) * softplus
        beta_all = _sigmoid(gl)
        g1, g2, g3 = _split3(g_all)
        gc_all = _dot(tril, g1) + _dot(tril, g2) + _dot(tril, g3)
        c1, c2, c3 = _split3(gc_all)
        packed = jnp.where(lane < hb, c1.astype(F32),
                           jnp.where(lane < 2 * hb, pltpu.roll(c2.astype(F32), hb, 1),
                                     pltpu.roll(c3.astype(F32), 2 * hb, 1))).astype(BF16)
        for i in range(hb):
            cols = slice(i * LANES, (i + 1) * LANES)
            gcol = gc_all[:, i:i + 1]
            bcol = beta_all[:, hb + i:hb + i + 1]
            pick = jnp.where((lane == i) | (lane == hb + i) | (lane == 2 * hb + i), 1.0, 0.0).astype(BF16)
            grow = _dot_nt(pick, packed)
            decay = jnp.exp(jnp.where(incl, gcol - grow, -jnp.inf))
            q = q_ref[rows, cols]
            k = k_ref[rows, cols]
            v = v_ref[rows, cols]
            kf = k.astype(F32)
            kbeta = kf * bcol
            kq = _dot_nt(jnp.concatenate([kbeta.astype(BF16), q], axis=0), k)
            a_kk = jnp.where(strict, kq[:L] * decay, 0.0)
            a_qk = jnp.where(incl, kq[L:] * decay, 0.0)
            t = _unit_lower_inverse(a_kk, row, col)
            eg = jnp.exp(gcol)
            rhs = jnp.concatenate([(v.astype(F32) * bcol).astype(BF16), (kbeta * eg).astype(BF16)], axis=1)
            uw = _dot(t.astype(BF16), rhs)
            state = state_ref[i]
            ws = _dot(jnp.concatenate([uw[:, LANES:].astype(BF16), (q.astype(F32) * eg).astype(BF16)], axis=0),
                      state.astype(BF16))
            v_new = uw[:, :LANES] - ws[:L]
            vnb = v_new.astype(BF16)
            o = ws[L:] + _dot(a_qk.astype(BF16), vnb)
            glast = gc_all[L - 1:L, i:i + 1]
            k_tail = (kf * jnp.exp(glast - gcol)).astype(BF16)
            state_ref[i] = state * jnp.exp(glast) + _dot_tn(k_tail, vnb)
            zf = z_ref[rows, cols].astype(F32)
            o_ref[rows, cols] = (_rms(o, nw) * (zf * _sigmoid(zf))).astype(o_ref.dtype)
        return carry

    lax.fori_loop(0, ts // L, chunk_step, 0)


def _gdn(gqkv, big, zcol0, lat, gcol0, alog_tab, dtb_tab, norm_w, batch, seq):
    nh, hb, ts, L = GDN_HEADS, GDN_HEADS_PER_STEP, GDN_SEQ_TILE, GDN_CHUNK
    ts = min(ts, seq)
    assert seq % ts == 0 and ts % L == 0 and nh % hb == 0
    ng = nh // hb
    ns = seq // ts
    w = hb * LANES
    kern = functools.partial(_gdn_kernel, hb=hb, ts=ts, chunk=L)
    head_block = lambda part: pl.BlockSpec((ts, w), lambda b, g, s: (b * ns + s, part * ng + g))
    return pl.pallas_call(
        kern,
        grid=(batch, ng, ns),
        in_specs=[head_block(0), head_block(1), head_block(2),
                  pl.BlockSpec((ts, w), lambda b, g, s: (b * ns + s, zcol0 // w + g)),
                  pl.BlockSpec((ts, LANES), lambda b, g, s: (b * ns + s, gcol0 // LANES + g)),
                  pl.BlockSpec((1, 1, LANES), lambda b, g, s: (g, 0, 0)),
                  pl.BlockSpec((1, 1, LANES), lambda b, g, s: (g, 0, 0)),
                  pl.BlockSpec((1, LANES), lambda b, g, s: (0, 0))],
        out_specs=pl.BlockSpec((ts, w), lambda b, g, s: (b * ns + s, g)),
        out_shape=jax.ShapeDtypeStruct((batch * seq, nh * GDN_DV), BF16),
        scratch_shapes=[pltpu.VMEM((hb, GDN_DK, GDN_DV), F32)],
        compiler_params=_cparams("parallel", "parallel", "arbitrary"),
        name="gated_delta_net",
    )(gqkv, gqkv, gqkv, big, lat, alog_tab, dtb_tab, norm_w.reshape(1, LANES))


def _mla_prep_kernel(ckv_ref, kr_ref, krs_ref, cq_ref, pos_ref, freq_ref, sign_ref, gq_ref, gkv_ref,
                     wqn_ref, wqp_ref, wqs_ref, wkn_ref, wv_ref,
                     qn_ref, qp_ref, kn_ref, kp_ref, v_ref, *, heads):
    ang = pos_ref[...].astype(F32) * freq_ref[...]
    cos = jnp.cos(ang)
    sin = jnp.sin(ang) * sign_ref[...]
    cqn = _rms(cq_ref[...], gq_ref[...]).astype(BF16)
    qn_ref[...] = _dot(cqn, wqn_ref[...]).astype(qn_ref.dtype)
    y = _dot(cqn, wqp_ref[...])
    ys = _dot(cqn, wqs_ref[...])
    for hh in range(heads):
        cols = slice(hh * LANES, (hh + 1) * LANES)
        qp_ref[:, cols] = (y[:, cols] * cos + ys[:, cols] * sin).astype(qp_ref.dtype)
    ckvn = _rms(ckv_ref[...], gkv_ref[...]).astype(BF16)
    kn_ref[...] = _dot(ckvn, wkn_ref[...]).astype(kn_ref.dtype)
    v_ref[...] = _dot(ckvn, wv_ref[...]).astype(v_ref.dtype)
    kp_ref[...] = (kr_ref[...] * cos + krs_ref[...] * sin).astype(kp_ref.dtype)


def _mla_prep(lat, pos_col2, freq, sign, gq, gkv, wqn, wqp, wqs, wkn, wv, tm=512):
    m = lat.shape[0]
    nh = MLA_HEADS
    width = nh * LANES
    const = lambda shape: pl.BlockSpec(shape, lambda i: (0, 0))
    out = pl.BlockSpec((tm, width), lambda i: (i, 0))
    qr, kvr = MLA_Q_RANK, MLA_KV_RANK
    return pl.pallas_call(
        functools.partial(_mla_prep_kernel, heads=nh),
        grid=(m // tm,),
        in_specs=[pl.BlockSpec((tm, kvr), lambda i: (i, 0)),
                  pl.BlockSpec((tm, LANES), lambda i: (i, kvr // LANES)),
                  pl.BlockSpec((tm, LANES), lambda i: (i, kvr // LANES + 1)),
                  pl.BlockSpec((tm, qr), lambda i: (i, 1)),
                  pl.BlockSpec((tm, 1), lambda i: (i, 0)),
                  const((1, LANES)), const((1, LANES)), const((1, qr)), const((1, kvr)),
                  const((qr, width)), const((qr, width)), const((qr, width)),
                  const((kvr, width)), const((kvr, width))],
        out_specs=[out, out, out, pl.BlockSpec((tm, LANES), lambda i: (i, 0)), out],
        out_shape=[jax.ShapeDtypeStruct((m, width), BF16), jax.ShapeDtypeStruct((m, width), BF16),
                   jax.ShapeDtypeStruct((m, width), BF16), jax.ShapeDtypeStruct((m, LANES), BF16),
                   jax.ShapeDtypeStruct((m, width), BF16)],
        compiler_params=_cparams("parallel"),
        name="mla_up_rope",
    )(lat, lat, lat, lat, pos_col2, freq, sign, gq.reshape(1, qr), gkv.reshape(1, kvr), wqn, wqp, wqs, wkn, wv)


def _mla_attn_kernel(qn_ref, qp_ref, kn_ref, kp_ref, v_ref, o_ref, m_ref, l_ref, acc_ref, *, tq, nq, scale):
    qi = pl.program_id(2)
    qn = qn_ref[...]
    qp = qp_ref[...]

    def scores(rows):
        return (_dot_nt(qn, kn_ref[rows, :]) + _dot_nt(qp, kp_ref[rows, :])) * scale

    r0 = pl.multiple_of(qi * tq, tq)
    s = scores(pl.ds(r0, tq))
    row = lax.broadcasted_iota(jnp.int32, s.shape, 0)
    col = lax.broadcasted_iota(jnp.int32, s.shape, 1)
    s = jnp.where(col <= row, s, -jnp.inf)
    _softmax_first(s, v_ref[pl.ds(r0, tq), :], m_ref, l_ref, acc_ref)

    for j in range(nq - 1):
        @pl.when(j < qi)
        def _(j=j):
            sl = slice(j * tq, (j + 1) * tq)
            _softmax_update(scores(sl), v_ref[sl, :], m_ref, l_ref, acc_ref)

    o_ref[...] = (acc_ref[...] / l_ref[...]).astype(o_ref.dtype)


def _mla_attention(qn, qp, kn, kp, v, batch, seq):
    nh, tq = MLA_HEADS, min(ATTN_TILE, seq)
    assert seq % tq == 0
    nq = seq // tq
    kern = functools.partial(_mla_attn_kernel, tq=tq, nq=nq, scale=(MLA_NOPE + MLA_ROPE) ** -0.5)
    qspec = pl.BlockSpec((tq, LANES), lambda b, h, i: (b * nq + i, h))
    kspec = pl.BlockSpec((seq, LANES), lambda b, h, i: (b, h))
    return pl.pallas_call(
        kern,
        grid=(batch, nh, nq),
        in_specs=[qspec, qspec, kspec, pl.BlockSpec((seq, LANES), lambda b, h, i: (b, 0)), kspec],
        out_specs=qspec,
        out_shape=jax.ShapeDtypeStruct((batch * seq, nh * MLA_V), BF16),
        scratch_shapes=[pltpu.VMEM((tq, 1), F32), pltpu.VMEM((tq, 1), F32), pltpu.VMEM((tq, MLA_V), F32)],
        compiler_params=_cparams("parallel", "parallel", "arbitrary"),
        name="mla_attention",
    )(qn, qp, kn, kp, v)


def _pad_cols(w, width):
    return jnp.pad(w, ((0, 0), (0, width - w.shape[1])))


def _swap_halves(w):
    half = w.shape[-1] // 2
    return jnp.concatenate([w[..., half:], w[..., :half]], axis=-1)


def _layer_params(i, w_in, gdn_a_log, gdn_dt_bias, mla_w_uq, mla_w_ukv):
    moba_w = MOBA_HEADS * HEAD_DIM
    gk, gv = GDN_HEADS * GDN_DK, GDN_HEADS * GDN_DV
    sizes = (moba_w, moba_w, moba_w, gk, gk, gv, GDN_HEADS, GDN_HEADS, gv, MLA_Q_RANK, MLA_KV_RANK, MLA_ROPE)
    off = np.concatenate([[0], np.cumsum(sizes)]).tolist()
    w = w_in[i]
    seg = lambda a: w[:, off[a]:off[a + 1]]
    w_big = jnp.concatenate([w[:, off[0]:off[6]], seg(8)], axis=1).astype(BF16)
    hb = GDN_HEADS_PER_STEP
    gate_blocks = []
    for g in range(GDN_HEADS // hb):
        gate_blocks.append(_pad_cols(jnp.concatenate(
            [seg(6)[:, g * hb:(g + 1) * hb], seg(7)[:, g * hb:(g + 1) * hb]], axis=1), LANES))
    w_lat = jnp.concatenate([seg(10), _pad_cols(seg(11), LANES), _pad_cols(_swap_halves(seg(11)), LANES), seg(9)]
                            + gate_blocks, axis=1).astype(BF16)
    alog_tab = jnp.pad(gdn_a_log[i].reshape(GDN_HEADS // hb, 1, hb), ((0, 0), (0, 0), (0, LANES - hb)))
    dtb_tab = jnp.pad(gdn_dt_bias[i].reshape(GDN_HEADS // hb, 1, hb), ((0, 0), (0, 0), (0, LANES - hb)))
    nh = MLA_HEADS
    wq = mla_w_uq[i].reshape(MLA_Q_RANK, nh, MLA_NOPE + MLA_ROPE)
    wqn = wq[:, :, :MLA_NOPE].reshape(MLA_Q_RANK, nh * MLA_NOPE).astype(BF16)
    pe = wq[:, :, MLA_NOPE:]
    padpe = lambda t: jnp.pad(t, ((0, 0), (0, 0), (0, LANES - MLA_ROPE))).reshape(MLA_Q_RANK, nh * LANES).astype(BF16)
    wkv = mla_w_ukv[i].reshape(MLA_KV_RANK, nh, MLA_NOPE + MLA_V)
    wkn = wkv[:, :, :MLA_NOPE].reshape(MLA_KV_RANK, nh * MLA_NOPE).astype(BF16)
    wv = wkv[:, :, MLA_NOPE:].reshape(MLA_KV_RANK, nh * MLA_V).astype(BF16)
    return dict(w_big=w_big, w_lat=w_lat, alog_tab=alog_tab, dtb_tab=dtb_tab,
                wqn=wqn, wqp=padpe(pe), wqs=padpe(_swap_halves(pe)), wkn=wkn, wv=wv)


def _rope_tables():
    half = MLA_ROPE // 2
    inv_freq = 1.0 / (ROPE_THETA ** (jnp.arange(half, dtype=F32) / half))
    freq = jnp.concatenate([inv_freq, inv_freq, jnp.zeros((LANES - MLA_ROPE,), F32)]).reshape(1, LANES)
    sign = jnp.concatenate([-jnp.ones((half,), F32), jnp.ones((half,), F32),
                            jnp.zeros((LANES - MLA_ROPE,), F32)]).reshape(1, LANES)
    return freq, sign


def kernel(x, p, positions, norm_mix_in, w_in, gdn_conv_w, gdn_a_log, gdn_dt_bias, gdn_norm_w, mla_q_norm_w, mla_w_uq, mla_kv_norm_w, mla_w_ukv, w_branch_gate, w_branch_a, w_branch_b, w_branch_c, w_out, norm_mix_out, norm_ffn_in, w_ffn_gate, w_ffn_up, w_ffn_down, norm_ffn_out, w_ple_gate, w_ple_proj):
    batch, seq, d = x.shape
    m = batch * seq
    depth = w_in.shape[0]
    moba_w = MOBA_HEADS * HEAD_DIM
    gdn_qkv_w = 2 * GDN_HEADS * GDN_DK + GDN_HEADS * GDN_DV
    zcol0 = 3 * moba_w + gdn_qkv_w
    gcol0 = MLA_KV_RANK + 2 * LANES + MLA_Q_RANK
    slopes = 2.0 ** (-8.0 * jnp.arange(1, MOBA_HEADS + 1, dtype=F32) / MOBA_HEADS)
    pos_col = positions.reshape(batch, seq, 1)
    pos_row = positions.reshape(batch, 1, seq)
    pos_col2 = positions.reshape(m, 1)
    freq, sign = _rope_tables()

    xf = x.reshape(m, d)
    h = _rmsnorm(xf, norm_mix_in[0])
    for i in range(depth):
        lp = _layer_params(i, w_in, gdn_a_log, gdn_dt_bias, mla_w_uq, mla_w_ukv)
        big = _matmul(h, lp["w_big"], BF16, 1024, 512, "in_proj_wide")
        lat = _matmul(h, lp["w_lat"], F32, 1024, 512, "in_proj_latent")
        y_a = _moba(big, pos_col, pos_row, slopes, batch, seq)
        gqkv = _gdn_prep(big, gdn_conv_w[i], 3 * moba_w // LANES, batch, seq)
        y_b = _gdn(gqkv, big, zcol0, lat, gcol0, lp["alog_tab"], lp["dtb_tab"], gdn_norm_w[i], batch, seq)
        qn, qp, kn, kp, v = _mla_prep(lat, pos_col2, freq, sign, mla_q_norm_w[i], mla_kv_norm_w[i],
                                      lp["wqn"], lp["wqp"], lp["wqs"], lp["wkn"], lp["wv"])
        y_c = _mla_attention(qn, qp, kn, kp, v, batch, seq)
        merged = _gated_merge(h, w_branch_gate[i].astype(BF16), y_a, y_b, y_c,
                              w_branch_a[i].astype(BF16), w_branch_b[i].astype(BF16), w_branch_c[i].astype(BF16))
        mixed = _matmul(merged, w_out[i].astype(BF16), F32, 1024, 512, "out_proj")
        xf, h = _residual_rmsnorm(xf, mixed, norm_mix_out[i], norm_ffn_in[i], True)
        act = _ffn_up(h, w_ffn_gate[i].astype(BF16), w_ffn_up[i].astype(BF16))
        dff = act.shape[1]
        f = _matmul_ksplit(act, w_ffn_down[i].astype(BF16), F32, 512, 1024, dff // 2, "ffn_down")
        xf, xb = _residual_rmsnorm(xf, f, norm_ffn_out[i], norm_ffn_out[i], False)
        xf = _ple(xb, xf, p[i].reshape(m, -1).astype(BF16), w_ple_gate[i].astype(BF16), w_ple_proj[i].astype(BF16))
        if i + 1 < depth:
            h = _rmsnorm(xf, norm_mix_in[i + 1])
    return xf.reshape(batch, seq, d)
```

```python
import functools

import numpy as np
import jax
import jax.numpy as jnp
from jax import lax
from jax.experimental import pallas as pl
from jax.experimental.pallas import tpu as pltpu

F32 = jnp.float32
BF16 = jnp.bfloat16

HEAD_DIM = 128
MOBA_HEADS = 8
MOBA_BLOCK = 256
MOBA_TOPK = 3
GDN_HEADS = 16
GDN_DK = 128
GDN_DV = 128
GDN_CONV = 4
GDN_CHUNK = 256
MLA_HEADS = 8
MLA_Q_RANK = 768
MLA_KV_RANK = 512
MLA_NOPE = 128
MLA_ROPE = 64
MLA_V = 128
ROPE_THETA = 10000.0
NORM_EPS = 1e-6

LANES = 128
VMEM_LIMIT_BYTES = 52 * 2 ** 20

GDN_HEADS_PER_STEP = 4
ATTN_TILE = 256
ATTN_GROUP = 4


def _cparams(*sem):
    return pltpu.CompilerParams(dimension_semantics=sem, vmem_limit_bytes=VMEM_LIMIT_BYTES)


def _dot(a, b):
    return jnp.dot(a, b, preferred_element_type=F32)


def _dot_nt(a, b):
    return lax.dot_general(a, b, (((1,), (1,)), ((), ())), preferred_element_type=F32)


def _dot_tn(a, b):
    return lax.dot_general(a, b, (((0,), (0,)), ((), ())), preferred_element_type=F32)


def _sigmoid(x):
    return jax.nn.sigmoid(x)


def _rms(x, gain):
    return x * lax.rsqrt(jnp.mean(x * x, axis=-1, keepdims=True) + NORM_EPS) * gain


def _rms_kernel(x_ref, g_ref, o_ref):
    o_ref[...] = _rms(x_ref[...], g_ref[...]).astype(o_ref.dtype)


def _rmsnorm(x, gain, tm=256):
    m, d = x.shape
    return pl.pallas_call(
        _rms_kernel,
        grid=(m // tm,),
        in_specs=[pl.BlockSpec((tm, d), lambda i: (i, 0)), pl.BlockSpec((1, d), lambda i: (0, 0))],
        out_specs=pl.BlockSpec((tm, d), lambda i: (i, 0)),
        out_shape=jax.ShapeDtypeStruct((m, d), BF16),
        compiler_params=_cparams("parallel"),
        name="rmsnorm",
    )(x, gain.reshape(1, d))


def _res_rms_kernel(x_ref, y_ref, g_ref, g2_ref, xo_ref, ho_ref, *, norm_next):
    xn = x_ref[...] + _rms(y_ref[...], g_ref[...])
    xo_ref[...] = xn
    if norm_next:
        ho_ref[...] = _rms(xn, g2_ref[...]).astype(ho_ref.dtype)
    else:
        ho_ref[...] = xn.astype(ho_ref.dtype)


def _residual_rmsnorm(x, y, gain, gain_next, norm_next, tm=256):
    m, d = x.shape
    row = pl.BlockSpec((tm, d), lambda i: (i, 0))
    vec = pl.BlockSpec((1, d), lambda i: (0, 0))
    return pl.pallas_call(
        functools.partial(_res_rms_kernel, norm_next=norm_next),
        grid=(m // tm,),
        in_specs=[row, row, vec, vec],
        out_specs=[row, row],
        out_shape=[jax.ShapeDtypeStruct((m, d), F32), jax.ShapeDtypeStruct((m, d), BF16)],
        compiler_params=_cparams("parallel"),
        name="residual_rmsnorm",
    )(x, y, gain.reshape(1, d), gain_next.reshape(1, d))


def _mm_kernel(a_ref, w_ref, o_ref):
    o_ref[...] = _dot(a_ref[...], w_ref[...]).astype(o_ref.dtype)


def _matmul(a, w, out_dtype, tm, tn, name):
    m, k = a.shape
    n = w.shape[1]
    return pl.pallas_call(
        _mm_kernel,
        grid=(m // tm, n // tn),
        in_specs=[pl.BlockSpec((tm, k), lambda i, j: (i, 0)), pl.BlockSpec((k, tn), lambda i, j: (0, j))],
        out_specs=pl.BlockSpec((tm, tn), lambda i, j: (i, j)),
        out_shape=jax.ShapeDtypeStruct((m, n), out_dtype),
        compiler_params=_cparams("parallel", "arbitrary"),
        name=name,
    )(a, w)


def _mm_acc_kernel(a_ref, w_ref, o_ref, acc_ref):
    kk = pl.program_id(2)

    @pl.when(kk == 0)
    def _():
        acc_ref[...] = jnp.zeros_like(acc_ref)

    acc_ref[...] += _dot(a_ref[...], w_ref[...])

    @pl.when(kk == pl.num_programs(2) - 1)
    def _():
        o_ref[...] = acc_ref[...].astype(o_ref.dtype)


def _matmul_ksplit(a, w, out_dtype, tm, tn, tk, name):
    m, k = a.shape
    n = w.shape[1]
    return pl.pallas_call(
        _mm_acc_kernel,
        grid=(m // tm, n // tn, k // tk),
        in_specs=[pl.BlockSpec((tm, tk), lambda i, j, kk: (i, kk)), pl.BlockSpec((tk, tn), lambda i, j, kk: (kk, j))],
        out_specs=pl.BlockSpec((tm, tn), lambda i, j, kk: (i, j)),
        out_shape=jax.ShapeDtypeStruct((m, n), out_dtype),
        scratch_shapes=[pltpu.VMEM((tm, tn), F32)],
        compiler_params=_cparams("parallel", "parallel", "arbitrary"),
        name=name,
    )(a, w)


def _merge_kernel(h_ref, wga_ref, wgb_ref, wgc_ref, ya_ref, yb_ref, yc_ref, wa_ref, wb_ref, wc_ref, o_ref):
    h = h_ref[...]
    out = _sigmoid(_dot(h, wga_ref[...])) * _dot(ya_ref[...], wa_ref[...])
    out += _sigmoid(_dot(h, wgb_ref[...])) * _dot(yb_ref[...], wb_ref[...])
    out += _sigmoid(_dot(h, wgc_ref[...])) * _dot(yc_ref[...], wc_ref[...])
    o_ref[...] = out.astype(o_ref.dtype)


def _gated_merge(h, w_gate, ya, yb, yc, wa, wb, wc, tm=512, tn=256):
    m, d = h.shape
    nj = d // tn
    res = lambda width: pl.BlockSpec((tm, width), lambda i, j: (i, 0))
    gate = lambda br: pl.BlockSpec((d, tn), lambda i, j: (0, br * nj + j))
    wcol = lambda width: pl.BlockSpec((width, tn), lambda i, j: (0, j))
    return pl.pallas_call(
        _merge_kernel,
        grid=(m // tm, nj),
        in_specs=[res(d), gate(0), gate(1), gate(2), res(ya.shape[1]), res(yb.shape[1]), res(yc.shape[1]),
                  wcol(wa.shape[0]), wcol(wb.shape[0]), wcol(wc.shape[0])],
        out_specs=pl.BlockSpec((tm, tn), lambda i, j: (i, j)),
        out_shape=jax.ShapeDtypeStruct((m, d), BF16),
        compiler_params=_cparams("parallel", "arbitrary"),
        name="gated_merge",
    )(h, w_gate, w_gate, w_gate, ya, yb, yc, wa, wb, wc)


def _ffn_up_kernel(h_ref, wg_ref, wu_ref, o_ref):
    h = h_ref[...]
    g = _dot(h, wg_ref[...])
    o_ref[...] = (g * _sigmoid(g) * _dot(h, wu_ref[...])).astype(o_ref.dtype)


def _ffn_up(h, wg, wu, tm=1024, tn=256):
    m, d = h.shape
    n = wg.shape[1]
    wspec = pl.BlockSpec((d, tn), lambda i, j: (0, j))
    return pl.pallas_call(
        _ffn_up_kernel,
        grid=(m // tm, n // tn),
        in_specs=[pl.BlockSpec((tm, d), lambda i, j: (i, 0)), wspec, wspec],
        out_specs=pl.BlockSpec((tm, tn), lambda i, j: (i, j)),
        out_shape=jax.ShapeDtypeStruct((m, n), BF16),
        compiler_params=_cparams("parallel", "arbitrary"),
        name="ffn_up",
    )(h, wg, wu)


def _ple_kernel(xb_ref, x_ref, p_ref, wg_ref, wp_ref, o_ref):
    gate = _sigmoid(_dot(xb_ref[...], wg_ref[...]))
    o_ref[...] = x_ref[...] + gate * _dot(p_ref[...], wp_ref[...])


def _ple(xb, x, p, wg, wp, tm=1024, tn=512):
    m, d = x.shape
    return pl.pallas_call(
        _ple_kernel,
        grid=(m // tm, d // tn),
        in_specs=[pl.BlockSpec((tm, d), lambda i, j: (i, 0)),
                  pl.BlockSpec((tm, tn), lambda i, j: (i, j)),
                  pl.BlockSpec((tm, p.shape[1]), lambda i, j: (i, 0)),
                  pl.BlockSpec((d, tn), lambda i, j: (0, j)),
                  pl.BlockSpec((p.shape[1], tn), lambda i, j: (0, j))],
        out_specs=pl.BlockSpec((tm, tn), lambda i, j: (i, j)),
        out_shape=jax.ShapeDtypeStruct((m, d), F32),
        compiler_params=_cparams("parallel", "arbitrary"),
        name="ple",
    )(xb, x, p, wg, wp)


def _flash_block_causal(n, t, score, value, m_ref, acc_ref, row_mask=None):
    row = lax.broadcasted_iota(jnp.int32, (t, t), 0)
    col = lax.broadcasted_iota(jnp.int32, (t, t), 1)
    causal = col <= row
    for g0 in range(0, n, ATTN_GROUP):
        tiles = range(g0, min(n, g0 + ATTN_GROUP))
        s = [jnp.where(causal, score(i, i), -jnp.inf) for i in tiles]
        m = [jnp.max(x, axis=-1, keepdims=True) for x in s]
        p = [jnp.exp(x - mm).astype(BF16) for x, mm in zip(s, m)]
        for i, mm, pp in zip(tiles, m, p):
            m_ref[i] = mm
            acc_ref[i] = _dot(pp, value(i))
    for j in range(n - 1):
        for g0 in range(j + 1, n, ATTN_GROUP):
            tiles = range(g0, min(n, g0 + ATTN_GROUP))
            s = [score(i, j) for i in tiles]
            if row_mask is not None:
                s = [x if row_mask(i, j) is None else jnp.where(row_mask(i, j), x, -jnp.inf) for i, x in zip(tiles, s)]
            m_old = [m_ref[i] for i in tiles]
            m_new = [jnp.maximum(mo, jnp.max(x, axis=-1, keepdims=True)) for mo, x in zip(m_old, s)]
            p = [jnp.exp(x - mn).astype(BF16) for x, mn in zip(s, m_new)]
            vj = value(j)
            for i, mo, mn, pp in zip(tiles, m_old, m_new, p):
                m_ref[i] = mn
                acc_ref[i] = jnp.exp(mo - mn) * acc_ref[i] + _dot(pp, vj)


def _flash_write(n, t, acc_ref, o_ref):
    for i in range(n):
        a = acc_ref[i]
        o_ref[i * t:(i + 1) * t, :] = (a[:, :LANES] / a[:, LANES:]).astype(o_ref.dtype)


def _moba_kernel(slope_ref, q_ref, k_ref, v_ref, pcol_ref, prow_ref, o_ref, km_ref, m_ref, acc_ref, *, blk, nb, topk, scale):
    slope = slope_ref[pl.program_id(1)]
    tile = lambda i: slice(i * blk, (i + 1) * blk)
    ones = jnp.ones((blk, LANES), BF16)

    km_ref[...] = jnp.zeros_like(km_ref)
    for j in range(nb):
        km_ref[j:j + 1, :] = jnp.mean(k_ref[tile(j), :].astype(F32), axis=0, keepdims=True)
    km = km_ref[...]
    km_hi = km.astype(BF16)
    km_lo = (km - km_hi.astype(F32)).astype(BF16)
    lane = lax.broadcasted_iota(jnp.int32, (blk, LANES), 1)

    def selection(i):
        if i <= topk:
            return [None] * i
        q = q_ref[tile(i), :]
        gate = _dot_nt(q, km_hi) + _dot_nt(q, km_lo)
        past = lane < i
        out = []
        for j in range(i):
            gj = gate[:, j:j + 1]
            beats = jnp.where(gate > gj, 1.0, jnp.where(gate == gj, jnp.where(lane < j, 1.0, 0.0), 0.0))
            rank = jnp.sum(jnp.where(past, beats, 0.0), axis=-1, keepdims=True)
            out.append(rank < topk)
        return out

    selected = [selection(i) for i in range(nb)]

    def score(i, j):
        s = _dot_nt(q_ref[tile(i), :], k_ref[tile(j), :]) * scale
        dist = jnp.abs(pcol_ref[0, tile(i), :] - prow_ref[0, :, tile(j)])
        return s - slope * dist.astype(F32)

    value = lambda j: jnp.concatenate([v_ref[tile(j), :], ones], axis=1)
    _flash_block_causal(nb, blk, score, value, m_ref, acc_ref, row_mask=lambda i, j: selected[i][j])
    _flash_write(nb, blk, acc_ref, o_ref)


def _moba(qkv, pos_col, pos_row, slopes, batch, seq):
    blk, nh, dh = MOBA_BLOCK, MOBA_HEADS, HEAD_DIM
    assert seq % blk == 0
    nb = seq // blk
    kern = functools.partial(_moba_kernel, blk=blk, nb=nb, topk=min(MOBA_TOPK, nb), scale=dh ** -0.5)
    return pl.pallas_call(
        kern,
        grid=(batch, nh),
        in_specs=[pl.BlockSpec(memory_space=pltpu.SMEM),
                  pl.BlockSpec((seq, dh), lambda b, h: (b, h)),
                  pl.BlockSpec((seq, dh), lambda b, h: (b, nh + h)),
                  pl.BlockSpec((seq, dh), lambda b, h: (b, 2 * nh + h)),
                  pl.BlockSpec((1, seq, 1), lambda b, h: (b, 0, 0)),
                  pl.BlockSpec((1, 1, seq), lambda b, h: (b, 0, 0))],
        out_specs=pl.BlockSpec((seq, dh), lambda b, h: (b, h)),
        out_shape=jax.ShapeDtypeStruct((batch * seq, nh * dh), BF16),
        scratch_shapes=[pltpu.VMEM((LANES, dh), F32), pltpu.VMEM((nb, blk, 1), F32),
                        pltpu.VMEM((nb, blk, 2 * LANES), F32)],
        compiler_params=_cparams("parallel", "parallel"),
        name="moba_attention",
    )(slopes, qkv, qkv, qkv, pos_col, pos_row)


def _gdn_prep_kernel(x_ref, w_ref, o_ref, xs_ref, *, seq, heads, slab, qscale):
    c = pl.program_id(1)
    pad = 8
    xs_ref[0:pad, :] = jnp.zeros((pad, LANES), F32)
    xs_ref[pad:pad + seq, :] = x_ref[...].astype(F32)
    w = w_ref[...]
    taps = w.shape[0]
    for r in range(0, seq, slab):
        y = jnp.zeros((slab, LANES), F32)
        for t in range(taps):
            start = pad + r - (taps - 1 - t)
            y = y + w[t:t + 1, :] * xs_ref[start:start + slab, :]
        y = y * _sigmoid(y)
        rs = lax.rsqrt(jnp.sum(y * y, axis=-1, keepdims=True) + NORM_EPS)
        fac = jnp.where(c < heads, rs * qscale, jnp.where(c < 2 * heads, rs, 1.0))
        o_ref[r:r + slab, :] = (y * fac).astype(o_ref.dtype)


def _gdn_prep(big, conv_w, col0, batch, seq):
    nh, dk = GDN_HEADS, GDN_DK
    ncol = conv_w.shape[1] // LANES
    kern = functools.partial(_gdn_prep_kernel, seq=seq, heads=nh, slab=256, qscale=dk ** -0.5)
    return pl.pallas_call(
        kern,
        grid=(batch, ncol),
        in_specs=[pl.BlockSpec((seq, LANES), lambda b, c: (b, col0 + c)),
                  pl.BlockSpec((conv_w.shape[0], LANES), lambda b, c: (0, c))],
        out_specs=pl.BlockSpec((seq, LANES), lambda b, c: (b, c)),
        out_shape=jax.ShapeDtypeStruct((batch * seq, conv_w.shape[1]), BF16),
        scratch_shapes=[pltpu.VMEM((seq + 8, LANES), F32)],
        compiler_params=_cparams("parallel", "parallel"),
        name="gdn_conv_norm",
    )(big, conv_w)


def _split3(x):
    p1 = x.astype(BF16)
    r1 = x - p1.astype(F32)
    p2 = r1.astype(BF16)
    p3 = (r1 - p2.astype(F32)).astype(BF16)
    return p1, p2, p3


def _inverse_level_masks(n):
    row = np.arange(n)[:, None]
    col = np.arange(n)[None, :]
    masks = [((row >> 3) == (col >> 3)) & (col < row)]
    shift = 4
    while (1 << (shift - 1)) < n:
        masks.append(((row >> shift) == (col >> shift)) & ((row >> (shift - 1)) != (col >> (shift - 1))) & (col < row))
        shift += 1
    return np.stack(masks).astype(np.float32)


def _unit_lower_inverse(a_list, mask_ref, eye):
    ab = [a.astype(BF16) for a in a_list]
    pb = [-(x * mask_ref[0]) for x in ab]
    p2b = [_dot(x, x).astype(BF16) for x in pb]
    t = [eye + x.astype(F32) for x in pb]
    t = [ti + _dot(ti.astype(BF16), x) for ti, x in zip(t, p2b)]
    p4b = [_dot(x, x).astype(BF16) for x in p2b]
    tb = [(ti + _dot(ti.astype(BF16), x)).astype(BF16) for ti, x in zip(t, p4b)]
    for lvl in range(1, mask_ref.shape[0]):
        xb = [_dot(x * mask_ref[lvl], ti).astype(BF16) for x, ti in zip(ab, tb)]
        tb = [ti - _dot(ti, x).astype(BF16) for ti, x in zip(tb, xb)]
    return tb


def _gdn_kernel(q_ref, k_ref, v_ref, z_ref, gt_ref, alog_ref, dtb_ref, nw_ref, mask_ref, o_ref, state_ref, *, hb, chunk):
    L = chunk
    heads = range(hb)

    @pl.when(pl.program_id(2) == 0)
    def _():
        state_ref[...] = jnp.zeros_like(state_ref)

    row = lax.broadcasted_iota(jnp.int32, (L, L), 0)
    col = lax.broadcasted_iota(jnp.int32, (L, L), 1)
    incl = col <= row
    strict = col < row
    tril = jnp.where(incl, 1.0, 0.0).astype(BF16)
    eye = jnp.where(row == col, 1.0, 0.0)
    lane = lax.broadcasted_iota(jnp.int32, (L, LANES), 1)

    gl = gt_ref[...]
    xg = gl + dtb_ref[0]
    softplus = jnp.maximum(xg, 0.0) + jnp.log1p(jnp.exp(-jnp.abs(xg)))
    g_all = -jnp.exp(alog_ref[0]) * softplus
    beta_all = _sigmoid(gl)
    g1, g2, g3 = _split3(g_all)
    gc_all = _dot(tril, g1) + _dot(tril, g2) + _dot(tril, g3)
    c1, c2, c3 = _split3(gc_all)
    packed = jnp.where(lane < hb, c1.astype(F32),
                       jnp.where(lane < 2 * hb, pltpu.roll(c2.astype(F32), hb, 1),
                                 pltpu.roll(c3.astype(F32), 2 * hb, 1))).astype(BF16)

    cols = [slice(i * LANES, (i + 1) * LANES) for i in heads]
    gcol = [gc_all[:, i:i + 1] for i in heads]
    bcol = [beta_all[:, hb + i:hb + i + 1] for i in heads]
    pick = [jnp.where((lane == i) | (lane == hb + i) | (lane == 2 * hb + i), 1.0, 0.0).astype(BF16) for i in heads]
    grow = [_dot_nt(pick[i], packed) for i in heads]
    decay = [jnp.exp(jnp.where(incl, gcol[i] - grow[i], -jnp.inf)) for i in heads]
    q = [q_ref[:, cols[i]] for i in heads]
    k = [k_ref[:, cols[i]] for i in heads]
    kf = [k[i].astype(F32) for i in heads]
    kbeta = [kf[i] * bcol[i] for i in heads]
    kq = [_dot_nt(jnp.concatenate([kbeta[i].astype(BF16), q[i]], axis=0), k[i]) for i in heads]
    a_kk = [jnp.where(strict, kq[i][:L] * decay[i], 0.0) for i in heads]
    a_qk = [jnp.where(incl, kq[i][L:] * decay[i], 0.0).astype(BF16) for i in heads]
    tb = _unit_lower_inverse(a_kk, mask_ref, eye)
    eg = [jnp.exp(gcol[i]) for i in heads]
    rhs = [jnp.concatenate([(v_ref[:, cols[i]].astype(F32) * bcol[i]).astype(BF16),
                            (kbeta[i] * eg[i]).astype(BF16)], axis=1) for i in heads]
    uw = [_dot(tb[i], rhs[i]) for i in heads]
    state = [state_ref[i] for i in heads]
    ws = [_dot(jnp.concatenate([uw[i][:, LANES:].astype(BF16), (q[i].astype(F32) * eg[i]).astype(BF16)], axis=0),
               state[i].astype(BF16)) for i in heads]
    vnb = [(uw[i][:, :LANES] - ws[i][:L]).astype(BF16) for i in heads]
    o = [ws[i][L:] + _dot(a_qk[i], vnb[i]) for i in heads]
    glast = [gc_all[L - 1:L, i:i + 1] for i in heads]
    k_tail = [(kf[i] * jnp.exp(glast[i] - gcol[i])).astype(BF16) for i in heads]
    new_state = [state[i] * jnp.exp(glast[i]) + _dot_tn(k_tail[i], vnb[i]) for i in heads]
    nw = nw_ref[...]
    for i in heads:
        state_ref[i] = new_state[i]
        zf = z_ref[:, cols[i]].astype(F32)
        o_ref[:, cols[i]] = (_rms(o[i], nw) * (zf * _sigmoid(zf))).astype(o_ref.dtype)


def _gdn(gqkv, big, zcol0, lat, gcol0, alog_tab, dtb_tab, norm_w, batch, seq):
    nh, hb = GDN_HEADS, GDN_HEADS_PER_STEP
    L = min(GDN_CHUNK, seq)
    assert seq % L == 0 and nh % hb == 0 and L % 8 == 0
    ng = nh // hb
    ns = seq // L
    w = hb * LANES
    masks = jnp.asarray(_inverse_level_masks(L), BF16)
    kern = functools.partial(_gdn_kernel, hb=hb, chunk=L)
    head_block = lambda part: pl.BlockSpec((L, w), lambda b, g, s: (b * ns + s, part * ng + g))
    return pl.pallas_call(
        kern,
        grid=(batch, ng, ns),
        in_specs=[head_block(0), head_block(1), head_block(2),
                  pl.BlockSpec((L, w), lambda b, g, s: (b * ns + s, zcol0 // w + g)),
                  pl.BlockSpec((L, LANES), lambda b, g, s: (b * ns + s, gcol0 // LANES + g)),
                  pl.BlockSpec((1, 1, LANES), lambda b, g, s: (g, 0, 0)),
                  pl.BlockSpec((1, 1, LANES), lambda b, g, s: (g, 0, 0)),
                  pl.BlockSpec((1, LANES), lambda b, g, s: (0, 0)),
                  pl.BlockSpec(masks.shape, lambda b, g, s: (0, 0, 0))],
        out_specs=pl.BlockSpec((L, w), lambda b, g, s: (b * ns + s, g)),
        out_shape=jax.ShapeDtypeStruct((batch * seq, nh * GDN_DV), BF16),
        scratch_shapes=[pltpu.VMEM((hb, GDN_DK, GDN_DV), F32)],
        compiler_params=_cparams("arbitrary", "arbitrary", "arbitrary"),
        name="gated_delta_net",
    )(gqkv, gqkv, gqkv, big, lat, alog_tab, dtb_tab, norm_w.reshape(1, LANES), masks)


def _mla_prep_kernel(ckv_ref, kr_ref, krs_ref, cq_ref, pos_ref, freq_ref, sign_ref, gq_ref, gkv_ref,
                     wqn_ref, wqp_ref, wqs_ref, wkn_ref, wv_ref,
                     qn_ref, qp_ref, kn_ref, kp_ref, v_ref, *, heads):
    ang = pos_ref[...].astype(F32) * freq_ref[...]
    cos = jnp.cos(ang)
    sin = jnp.sin(ang) * sign_ref[...]
    cqn = _rms(cq_ref[...], gq_ref[...]).astype(BF16)
    qn_ref[...] = _dot(cqn, wqn_ref[...]).astype(qn_ref.dtype)
    y = _dot(cqn, wqp_ref[...])
    ys = _dot(cqn, wqs_ref[...])
    for hh in range(heads):
        cols = slice(hh * LANES, (hh + 1) * LANES)
        qp_ref[:, cols] = (y[:, cols] * cos + ys[:, cols] * sin).astype(qp_ref.dtype)
    ckvn = _rms(ckv_ref[...], gkv_ref[...]).astype(BF16)
    kn_ref[...] = _dot(ckvn, wkn_ref[...]).astype(kn_ref.dtype)
    v_ref[...] = _dot(ckvn, wv_ref[...]).astype(v_ref.dtype)
    kp_ref[...] = (kr_ref[...] * cos + krs_ref[...] * sin).astype(kp_ref.dtype)


def _mla_prep(lat, pos_col2, freq, sign, gq, gkv, wqn, wqp, wqs, wkn, wv, tm=512):
    m = lat.shape[0]
    nh = MLA_HEADS
    width = nh * LANES
    const = lambda shape: pl.BlockSpec(shape, lambda i: (0, 0))
    out = pl.BlockSpec((tm, width), lambda i: (i, 0))
    qr, kvr = MLA_Q_RANK, MLA_KV_RANK
    return pl.pallas_call(
        functools.partial(_mla_prep_kernel, heads=nh),
        grid=(m // tm,),
        in_specs=[pl.BlockSpec((tm, kvr), lambda i: (i, 0)),
                  pl.BlockSpec((tm, LANES), lambda i: (i, kvr // LANES)),
                  pl.BlockSpec((tm, LANES), lambda i: (i, kvr // LANES + 1)),
                  pl.BlockSpec((tm, qr), lambda i: (i, 1)),
                  pl.BlockSpec((tm, 1), lambda i: (i, 0)),
                  const((1, LANES)), const((1, LANES)), const((1, qr)), const((1, kvr)),
                  const((qr, width)), const((qr, width)), const((qr, width)),
                  const((kvr, width)), const((kvr, width))],
        out_specs=[out, out, out, pl.BlockSpec((tm, LANES), lambda i: (i, 0)), out],
        out_shape=[jax.ShapeDtypeStruct((m, width), BF16), jax.ShapeDtypeStruct((m, width), BF16),
                   jax.ShapeDtypeStruct((m, width), BF16), jax.ShapeDtypeStruct((m, LANES), BF16),
                   jax.ShapeDtypeStruct((m, width), BF16)],
        compiler_params=_cparams("parallel"),
        name="mla_up_rope",
    )(lat, lat, lat, lat, pos_col2, freq, sign, gq.reshape(1, qr), gkv.reshape(1, kvr), wqn, wqp, wqs, wkn, wv)


def _mla_attn_kernel(qn_ref, qp_ref, kn_ref, kp_ref, v_ref, o_ref, m_ref, acc_ref, *, t, n, scale):
    tile = lambda i: slice(i * t, (i + 1) * t)
    ones = jnp.ones((t, LANES), BF16)

    def score(i, j):
        q = jnp.concatenate([qn_ref[tile(i), :], qp_ref[tile(i), :]], axis=1)
        k = jnp.concatenate([kn_ref[tile(j), :], kp_ref[tile(j), :]], axis=1)
        return _dot_nt(q, k) * scale

    value = lambda j: jnp.concatenate([v_ref[tile(j), :], ones], axis=1)
    _flash_block_causal(n, t, score, value, m_ref, acc_ref)
    _flash_write(n, t, acc_ref, o_ref)


def _mla_attention(qn, qp, kn, kp, v, batch, seq):
    nh, t = MLA_HEADS, min(ATTN_TILE, seq)
    assert seq % t == 0
    n = seq // t
    kern = functools.partial(_mla_attn_kernel, t=t, n=n, scale=(MLA_NOPE + MLA_ROPE) ** -0.5)
    head = pl.BlockSpec((seq, LANES), lambda b, h: (b, h))
    return pl.pallas_call(
        kern,
        grid=(batch, nh),
        in_specs=[head, head, head, pl.BlockSpec((seq, LANES), lambda b, h: (b, 0)), head],
        out_specs=head,
        out_shape=jax.ShapeDtypeStruct((batch * seq, nh * MLA_V), BF16),
        scratch_shapes=[pltpu.VMEM((n, t, 1), F32), pltpu.VMEM((n, t, 2 * LANES), F32)],
        compiler_params=_cparams("parallel", "parallel"),
        name="mla_attention",
    )(qn, qp, kn, kp, v)


def _pad_cols(w, width):
    return jnp.pad(w, ((0, 0), (0, width - w.shape[1])))


def _swap_halves(w):
    half = w.shape[-1] // 2
    return jnp.concatenate([w[..., half:], w[..., :half]], axis=-1)


def _layer_params(i, w_in, gdn_a_log, gdn_dt_bias, mla_w_uq, mla_w_ukv):
    moba_w = MOBA_HEADS * HEAD_DIM
    gk, gv = GDN_HEADS * GDN_DK, GDN_HEADS * GDN_DV
    sizes = (moba_w, moba_w, moba_w, gk, gk, gv, GDN_HEADS, GDN_HEADS, gv, MLA_Q_RANK, MLA_KV_RANK, MLA_ROPE)
    off = np.concatenate([[0], np.cumsum(sizes)]).tolist()
    w = w_in[i]
    seg = lambda a: w[:, off[a]:off[a + 1]]
    w_big = jnp.concatenate([w[:, off[0]:off[6]], seg(8)], axis=1).astype(BF16)
    hb = GDN_HEADS_PER_STEP
    gate_blocks = []
    for g in range(GDN_HEADS // hb):
        gate_blocks.append(_pad_cols(jnp.concatenate(
            [seg(6)[:, g * hb:(g + 1) * hb], seg(7)[:, g * hb:(g + 1) * hb]], axis=1), LANES))
    w_lat = jnp.concatenate([seg(10), _pad_cols(seg(11), LANES), _pad_cols(_swap_halves(seg(11)), LANES), seg(9)]
                            + gate_blocks, axis=1).astype(BF16)
    alog_tab = jnp.pad(gdn_a_log[i].reshape(GDN_HEADS // hb, 1, hb), ((0, 0), (0, 0), (0, LANES - hb)))
    dtb_tab = jnp.pad(gdn_dt_bias[i].reshape(GDN_HEADS // hb, 1, hb), ((0, 0), (0, 0), (0, LANES - hb)))
    nh = MLA_HEADS
    wq = mla_w_uq[i].reshape(MLA_Q_RANK, nh, MLA_NOPE + MLA_ROPE)
    wqn = wq[:, :, :MLA_NOPE].reshape(MLA_Q_RANK, nh * MLA_NOPE).astype(BF16)
    pe = wq[:, :, MLA_NOPE:]
    padpe = lambda t: jnp.pad(t, ((0, 0), (0, 0), (0, LANES - MLA_ROPE))).reshape(MLA_Q_RANK, nh * LANES).astype(BF16)
    wkv = mla_w_ukv[i].reshape(MLA_KV_RANK, nh, MLA_NOPE + MLA_V)
    wkn = wkv[:, :, :MLA_NOPE].reshape(MLA_KV_RANK, nh * MLA_NOPE).astype(BF16)
    wv = wkv[:, :, MLA_NOPE:].reshape(MLA_KV_RANK, nh * MLA_V).astype(BF16)
    return dict(w_big=w_big, w_lat=w_lat, alog_tab=alog_tab, dtb_tab=dtb_tab,
                wqn=wqn, wqp=padpe(pe), wqs=padpe(_swap_halves(pe)), wkn=wkn, wv=wv)


def _rope_tables():
    half = MLA_ROPE // 2
    inv_freq = 1.0 / (ROPE_THETA ** (jnp.arange(half, dtype=F32) / half))
    freq = jnp.concatenate([inv_freq, inv_freq, jnp.zeros((LANES - MLA_ROPE,), F32)]).reshape(1, LANES)
    sign = jnp.concatenate([-jnp.ones((half,), F32), jnp.ones((half,), F32),
                            jnp.zeros((LANES - MLA_ROPE,), F32)]).reshape(1, LANES)
    return freq, sign


def kernel(x, p, positions, norm_mix_in, w_in, gdn_conv_w, gdn_a_log, gdn_dt_bias, gdn_norm_w, mla_q_norm_w, mla_w_uq, mla_kv_norm_w, mla_w_ukv, w_branch_gate, w_branch_a, w_branch_b, w_branch_c, w_out, norm_mix_out, norm_ffn_in, w_ffn_gate, w_ffn_up, w_ffn_down, norm_ffn_out, w_ple_gate, w_ple_proj):
    batch, seq, d = x.shape
    m = batch * seq
    depth = w_in.shape[0]
    moba_w = MOBA_HEADS * HEAD_DIM
    gdn_qkv_w = 2 * GDN_HEADS * GDN_DK + GDN_HEADS * GDN_DV
    zcol0 = 3 * moba_w + gdn_qkv_w
    gcol0 = MLA_KV_RANK + 2 * LANES + MLA_Q_RANK
    slopes = 2.0 ** (-8.0 * jnp.arange(1, MOBA_HEADS + 1, dtype=F32) / MOBA_HEADS)
    pos_col = positions.reshape(batch, seq, 1)
    pos_row = positions.reshape(batch, 1, seq)
    pos_col2 = positions.reshape(m, 1)
    freq, sign = _rope_tables()

    xf = x.reshape(m, d)
    h = _rmsnorm(xf, norm_mix_in[0])
    for i in range(depth):
        lp = _layer_params(i, w_in, gdn_a_log, gdn_dt_bias, mla_w_uq, mla_w_ukv)
        big = _matmul(h, lp["w_big"], BF16, 1024, 512, "in_proj_wide")
        lat = _matmul(h, lp["w_lat"], F32, 1024, 512, "in_proj_latent")
        y_a = _moba(big, pos_col, pos_row, slopes, batch, seq)
        gqkv = _gdn_prep(big, gdn_conv_w[i], 3 * moba_w // LANES, batch, seq)
        y_b = _gdn(gqkv, big, zcol0, lat, gcol0, lp["alog_tab"], lp["dtb_tab"], gdn_norm_w[i], batch, seq)
        qn, qp, kn, kp, v = _mla_prep(lat, pos_col2, freq, sign, mla_q_norm_w[i], mla_kv_norm_w[i],
                                      lp["wqn"], lp["wqp"], lp["wqs"], lp["wkn"], lp["wv"])
        y_c = _mla_attention(qn, qp, kn, kp, v, batch, seq)
        merged = _gated_merge(h, w_branch_gate[i].astype(BF16), y_a, y_b, y_c,
                              w_branch_a[i].astype(BF16), w_branch_b[i].astype(BF16), w_branch_c[i].astype(BF16))
        mixed = _matmul(merged, w_out[i].astype(BF16), F32, 1024, 512, "out_proj")
        xf, h = _residual_rmsnorm(xf, mixed, norm_mix_out[i], norm_ffn_in[i], True)
        act = _ffn_up(h, w_ffn_gate[i].astype(BF16), w_ffn_up[i].astype(BF16))
        dff = act.shape[1]
        f = _matmul_ksplit(act, w_ffn_down[i].astype(BF16), F32, 512, 1024, dff // 2, "ffn_down")
        xf, xb = _residual_rmsnorm(xf, f, norm_ffn_out[i], norm_ffn_out[i], False)
        xf = _ple(xb, xf, p[i].reshape(m, -1).astype(BF16), w_ple_gate[i].astype(BF16), w_ple_proj[i].astype(BF16))
        if i + 1 < depth:
            h = _rmsnorm(xf, norm_mix_in[i + 1])
    return xf.reshape(batch, seq, d)
```

```python
import functools

import numpy as np
import jax
import jax.numpy as jnp
from jax import lax
from jax.experimental import pallas as pl
from jax.experimental.pallas import tpu as pltpu

F32 = jnp.float32
BF16 = jnp.bfloat16

HEAD_DIM = 128
MOBA_HEADS = 8
MOBA_BLOCK = 256
MOBA_TOPK = 3
GDN_HEADS = 16
GDN_DK = 128
GDN_DV = 128
GDN_CONV = 4
GDN_CHUNK = 256
MLA_HEADS = 8
MLA_Q_RANK = 768
MLA_KV_RANK = 512
MLA_NOPE = 128
MLA_ROPE = 64
MLA_V = 128
ROPE_THETA = 10000.0
NORM_EPS = 1e-6

LANES = 128
VMEM_LIMIT_BYTES = 52 * 2 ** 20

GDN_HEADS_PER_STEP = 4
ATTN_TILE = 256
ATTN_GROUP = 4


def _cparams(*sem):
    return pltpu.CompilerParams(dimension_semantics=sem, vmem_limit_bytes=VMEM_LIMIT_BYTES)


def _dot(a, b):
    return jnp.dot(a, b, preferred_element_type=F32)


def _dot_nt(a, b):
    return lax.dot_general(a, b, (((1,), (1,)), ((), ())), preferred_element_type=F32)


def _dot_tn(a, b):
    return lax.dot_general(a, b, (((0,), (0,)), ((), ())), preferred_element_type=F32)


def _sigmoid(x):
    return jax.nn.sigmoid(x)


def _rms(x, gain):
    return x * lax.rsqrt(jnp.mean(x * x, axis=-1, keepdims=True) + NORM_EPS) * gain


def _rms_kernel(x_ref, g_ref, o_ref):
    o_ref[...] = _rms(x_ref[...], g_ref[...]).astype(o_ref.dtype)


def _rmsnorm(x, gain, tm=256):
    m, d = x.shape
    return pl.pallas_call(
        _rms_kernel,
        grid=(m // tm,),
        in_specs=[pl.BlockSpec((tm, d), lambda i: (i, 0)), pl.BlockSpec((1, d), lambda i: (0, 0))],
        out_specs=pl.BlockSpec((tm, d), lambda i: (i, 0)),
        out_shape=jax.ShapeDtypeStruct((m, d), BF16),
        compiler_params=_cparams("parallel"),
        name="rmsnorm",
    )(x, gain.reshape(1, d))


def _res_rms_kernel(x_ref, y_ref, g_ref, g2_ref, xo_ref, ho_ref, *, norm_next):
    xn = x_ref[...] + _rms(y_ref[...].astype(F32), g_ref[...])
    xo_ref[...] = xn
    if norm_next:
        ho_ref[...] = _rms(xn, g2_ref[...]).astype(ho_ref.dtype)
    else:
        ho_ref[...] = xn.astype(ho_ref.dtype)


def _residual_rmsnorm(x, y, gain, gain_next, norm_next, tm=256):
    m, d = x.shape
    row = pl.BlockSpec((tm, d), lambda i: (i, 0))
    vec = pl.BlockSpec((1, d), lambda i: (0, 0))
    return pl.pallas_call(
        functools.partial(_res_rms_kernel, norm_next=norm_next),
        grid=(m // tm,),
        in_specs=[row, row, vec, vec],
        out_specs=[row, row],
        out_shape=[jax.ShapeDtypeStruct((m, d), F32), jax.ShapeDtypeStruct((m, d), BF16)],
        compiler_params=_cparams("parallel"),
        name="residual_rmsnorm",
    )(x, y, gain.reshape(1, d), gain_next.reshape(1, d))


def _wspec(rows, tn, layer, col=lambda j: j):
    return pl.BlockSpec((None, rows, tn), lambda j, i: (layer, 0, col(j)), pipeline_mode=pl.Buffered(1))


def _stacked(w):
    return w if w.ndim == 3 else w.reshape((1,) + w.shape)


def _cast_weights(pairs):
    @pl.when(pl.program_id(1) == 0)
    def _():
        for src, dst in pairs:
            dst[...] = src[...].astype(dst.dtype)


def _mm_kernel(a_ref, w_ref, o_ref, wb_ref):
    _cast_weights([(w_ref, wb_ref)])
    o_ref[...] = _dot(a_ref[...], wb_ref[...]).astype(o_ref.dtype)


def _matmul(a, w, layer, n, out_dtype, tm, tn, name):
    m, k = a.shape
    return pl.pallas_call(
        _mm_kernel,
        grid=(n // tn, m // tm),
        in_specs=[pl.BlockSpec((tm, k), lambda j, i: (i, 0)), _wspec(k, tn, layer)],
        out_specs=pl.BlockSpec((tm, tn), lambda j, i: (i, j)),
        out_shape=jax.ShapeDtypeStruct((m, n), out_dtype),
        scratch_shapes=[pltpu.VMEM((k, tn), BF16)],
        compiler_params=_cparams("parallel", "arbitrary"),
        name=name,
    )(a, _stacked(w))


def _mm_acc_kernel(a_ref, w_ref, o_ref, acc_ref):
    kk = pl.program_id(2)

    @pl.when(kk == 0)
    def _():
        acc_ref[...] = jnp.zeros_like(acc_ref)

    acc_ref[...] += _dot(a_ref[...], w_ref[...])

    @pl.when(kk == pl.num_programs(2) - 1)
    def _():
        o_ref[...] = acc_ref[...].astype(o_ref.dtype)


def _matmul_ksplit(a, w, out_dtype, tm, tn, tk, name):
    m, k = a.shape
    n = w.shape[1]
    return pl.pallas_call(
        _mm_acc_kernel,
        grid=(m // tm, n // tn, k // tk),
        in_specs=[pl.BlockSpec((tm, tk), lambda i, j, kk: (i, kk)), pl.BlockSpec((tk, tn), lambda i, j, kk: (kk, j))],
        out_specs=pl.BlockSpec((tm, tn), lambda i, j, kk: (i, j)),
        out_shape=jax.ShapeDtypeStruct((m, n), out_dtype),
        scratch_shapes=[pltpu.VMEM((tm, tn), F32)],
        compiler_params=_cparams("parallel", "parallel", "arbitrary"),
        name=name,
    )(a, w)


def _merge_kernel(h_ref, ya_ref, yb_ref, yc_ref, wga_ref, wgb_ref, wgc_ref, wa_ref, wb_ref, wc_ref, o_ref,
                  ga_s, gb_s, gc_s, a_s, b_s, c_s):
    _cast_weights([(wga_ref, ga_s), (wgb_ref, gb_s), (wgc_ref, gc_s), (wa_ref, a_s), (wb_ref, b_s), (wc_ref, c_s)])
    h = h_ref[...]
    out = _sigmoid(_dot(h, ga_s[...])) * _dot(ya_ref[...], a_s[...])
    out += _sigmoid(_dot(h, gb_s[...])) * _dot(yb_ref[...], b_s[...])
    out += _sigmoid(_dot(h, gc_s[...])) * _dot(yc_ref[...], c_s[...])
    o_ref[...] = out.astype(o_ref.dtype)


def _gated_merge(h, w_gate, ya, yb, yc, wa, wb, wc, layer, tm=512, tn=256):
    m, d = h.shape
    nj = d // tn
    res = lambda width: pl.BlockSpec((tm, width), lambda j, i: (i, 0))
    gate = lambda br: _wspec(d, tn, layer, lambda j: br * nj + j)
    widths = (wa.shape[1], wb.shape[1], wc.shape[1])
    return pl.pallas_call(
        _merge_kernel,
        grid=(nj, m // tm),
        in_specs=[res(d), res(widths[0]), res(widths[1]), res(widths[2]), gate(0), gate(1), gate(2),
                  _wspec(widths[0], tn, layer), _wspec(widths[1], tn, layer), _wspec(widths[2], tn, layer)],
        out_specs=pl.BlockSpec((tm, tn), lambda j, i: (i, j)),
        out_shape=jax.ShapeDtypeStruct((m, d), BF16),
        scratch_shapes=[pltpu.VMEM((d, tn), BF16)] * 3 + [pltpu.VMEM((w, tn), BF16) for w in widths],
        compiler_params=_cparams("parallel", "arbitrary"),
        name="gated_merge",
    )(h, ya, yb, yc, w_gate, w_gate, w_gate, wa, wb, wc)


def _ffn_up_kernel(h_ref, wg_ref, wu_ref, o_ref, g_s, u_s):
    _cast_weights([(wg_ref, g_s), (wu_ref, u_s)])
    h = h_ref[...]
    g = _dot(h, g_s[...])
    o_ref[...] = (g * _sigmoid(g) * _dot(h, u_s[...])).astype(o_ref.dtype)


def _ffn_up(h, wg, wu, layer, tm=1024, tn=256):
    m, d = h.shape
    n = wg.shape[2]
    return pl.pallas_call(
        _ffn_up_kernel,
        grid=(n // tn, m // tm),
        in_specs=[pl.BlockSpec((tm, d), lambda j, i: (i, 0)), _wspec(d, tn, layer), _wspec(d, tn, layer)],
        out_specs=pl.BlockSpec((tm, tn), lambda j, i: (i, j)),
        out_shape=jax.ShapeDtypeStruct((m, n), BF16),
        scratch_shapes=[pltpu.VMEM((d, tn), BF16)] * 2,
        compiler_params=_cparams("parallel", "arbitrary"),
        name="ffn_up",
    )(h, wg, wu)


def _ple_kernel(xb_ref, x_ref, p_ref, wg_ref, wp_ref, o_ref, g_s, p_s):
    _cast_weights([(wg_ref, g_s), (wp_ref, p_s)])
    gate = _sigmoid(_dot(xb_ref[...], g_s[...]))
    o_ref[...] = x_ref[...] + gate * _dot(p_ref[...], p_s[...])


def _ple(xb, x, p, wg, wp, layer, tm=1024, tn=512):
    m, d = x.shape
    dp = p.shape[1]
    return pl.pallas_call(
        _ple_kernel,
        grid=(d // tn, m // tm),
        in_specs=[pl.BlockSpec((tm, d), lambda j, i: (i, 0)),
                  pl.BlockSpec((tm, tn), lambda j, i: (i, j)),
                  pl.BlockSpec((tm, dp), lambda j, i: (i, 0)),
                  _wspec(d, tn, layer), _wspec(dp, tn, layer)],
        out_specs=pl.BlockSpec((tm, tn), lambda j, i: (i, j)),
        out_shape=jax.ShapeDtypeStruct((m, d), F32),
        scratch_shapes=[pltpu.VMEM((d, tn), BF16), pltpu.VMEM((dp, tn), BF16)],
        compiler_params=_cparams("parallel", "arbitrary"),
        name="ple",
    )(xb, x, p, wg, wp)


def _flash_block_causal(n, t, score, value, m_ref, acc_ref, row_mask=None):
    row = lax.broadcasted_iota(jnp.int32, (t, t), 0)
    col = lax.broadcasted_iota(jnp.int32, (t, t), 1)
    causal = col <= row
    for g0 in range(0, n, ATTN_GROUP):
        tiles = range(g0, min(n, g0 + ATTN_GROUP))
        s = [jnp.where(causal, score(i, i), -jnp.inf) for i in tiles]
        m = [jnp.max(x, axis=-1, keepdims=True) for x in s]
        p = [jnp.exp(x - mm).astype(BF16) for x, mm in zip(s, m)]
        for i, mm, pp in zip(tiles, m, p):
            m_ref[i] = mm
            acc_ref[i] = _dot(pp, value(i))
    for j in range(n - 1):
        for g0 in range(j + 1, n, ATTN_GROUP):
            tiles = range(g0, min(n, g0 + ATTN_GROUP))
            s = [score(i, j) for i in tiles]
            if row_mask is not None:
                s = [x if row_mask(i, j) is None else jnp.where(row_mask(i, j), x, -jnp.inf) for i, x in zip(tiles, s)]
            m_old = [m_ref[i] for i in tiles]
            m_new = [jnp.maximum(mo, jnp.max(x, axis=-1, keepdims=True)) for mo, x in zip(m_old, s)]
            p = [jnp.exp(x - mn).astype(BF16) for x, mn in zip(s, m_new)]
            vj = value(j)
            for i, mo, mn, pp in zip(tiles, m_old, m_new, p):
                m_ref[i] = mn
                acc_ref[i] = jnp.exp(mo - mn) * acc_ref[i] + _dot(pp, vj)


def _flash_write(n, t, acc_ref, o_ref):
    for i in range(n):
        a = acc_ref[i]
        o_ref[i * t:(i + 1) * t, :] = (a[:, :LANES] / a[:, LANES:]).astype(o_ref.dtype)


def _moba_kernel(slope_ref, q_ref, k_ref, v_ref, pcol_ref, prow_ref, o_ref, km_ref, m_ref, acc_ref, *, blk, nb, topk, scale):
    slope = slope_ref[pl.program_id(1)]
    tile = lambda i: slice(i * blk, (i + 1) * blk)
    ones = jnp.ones((blk, LANES), BF16)

    km_ref[...] = jnp.zeros_like(km_ref)
    for j in range(nb):
        km_ref[j:j + 1, :] = jnp.mean(k_ref[tile(j), :].astype(F32), axis=0, keepdims=True)
    km = km_ref[...]
    km_hi = km.astype(BF16)
    km_lo = (km - km_hi.astype(F32)).astype(BF16)
    lane = lax.broadcasted_iota(jnp.int32, (blk, LANES), 1)

    def selection(i):
        if i <= topk:
            return [None] * i
        q = q_ref[tile(i), :]
        gate = _dot_nt(q, km_hi) + _dot_nt(q, km_lo)
        past = lane < i
        out = []
        for j in range(i):
            gj = gate[:, j:j + 1]
            beats = jnp.where(gate > gj, 1.0, jnp.where(gate == gj, jnp.where(lane < j, 1.0, 0.0), 0.0))
            rank = jnp.sum(jnp.where(past, beats, 0.0), axis=-1, keepdims=True)
            out.append(rank < topk)
        return out

    selected = [selection(i) for i in range(nb)]

    def score(i, j):
        s = _dot_nt(q_ref[tile(i), :], k_ref[tile(j), :]) * scale
        dist = jnp.abs(pcol_ref[0, tile(i), :] - prow_ref[0, :, tile(j)])
        return s - slope * dist.astype(F32)

    value = lambda j: jnp.concatenate([v_ref[tile(j), :], ones], axis=1)
    _flash_block_causal(nb, blk, score, value, m_ref, acc_ref, row_mask=lambda i, j: selected[i][j])
    _flash_write(nb, blk, acc_ref, o_ref)


def _moba(qkv, pos_col, pos_row, slopes, batch, seq):
    blk, nh, dh = MOBA_BLOCK, MOBA_HEADS, HEAD_DIM
    assert seq % blk == 0
    nb = seq // blk
    kern = functools.partial(_moba_kernel, blk=blk, nb=nb, topk=min(MOBA_TOPK, nb), scale=dh ** -0.5)
    return pl.pallas_call(
        kern,
        grid=(batch, nh),
        in_specs=[pl.BlockSpec(memory_space=pltpu.SMEM),
                  pl.BlockSpec((seq, dh), lambda b, h: (b, h)),
                  pl.BlockSpec((seq, dh), lambda b, h: (b, nh + h)),
                  pl.BlockSpec((seq, dh), lambda b, h: (b, 2 * nh + h)),
                  pl.BlockSpec((1, seq, 1), lambda b, h: (b, 0, 0)),
                  pl.BlockSpec((1, 1, seq), lambda b, h: (b, 0, 0))],
        out_specs=pl.BlockSpec((seq, dh), lambda b, h: (b, h)),
        out_shape=jax.ShapeDtypeStruct((batch * seq, nh * dh), BF16),
        scratch_shapes=[pltpu.VMEM((LANES, dh), F32), pltpu.VMEM((nb, blk, 1), F32),
                        pltpu.VMEM((nb, blk, 2 * LANES), F32)],
        compiler_params=_cparams("parallel", "parallel"),
        name="moba_attention",
    )(slopes, qkv, qkv, qkv, pos_col, pos_row)


def _gdn_prep_kernel(x_ref, w_ref, o_ref, xs_ref, *, seq, part_blocks, slab, qscale):
    c = pl.program_id(1)
    pad = 8
    width = x_ref.shape[1]
    xs_ref[0:pad, :] = jnp.zeros((pad, width), F32)
    xs_ref[pad:pad + seq, :] = x_ref[...].astype(F32)
    w = w_ref[...]
    taps = w.shape[0]
    for r in range(0, seq, slab):
        win = xs_ref[r:r + pad + slab, :]
        y = w[0:1, :] * win
        for t in range(1, taps):
            y = pltpu.roll(y, 1, 0) + w[t:t + 1, :] * win
        y = y[pad:, :]
        y = y * _sigmoid(y)
        for g in range(width // LANES):
            yg = y[:, g * LANES:(g + 1) * LANES]
            rs = lax.rsqrt(jnp.sum(yg * yg, axis=-1, keepdims=True) + NORM_EPS)
            fac = jnp.where(c < part_blocks, rs * qscale, jnp.where(c < 2 * part_blocks, rs, 1.0))
            o_ref[r:r + slab, g * LANES:(g + 1) * LANES] = (yg * fac).astype(o_ref.dtype)


def _gdn_prep(big, conv_w, col0, batch, seq, width=4 * LANES):
    nh, dk = GDN_HEADS, GDN_DK
    ncol = conv_w.shape[1] // width
    kern = functools.partial(_gdn_prep_kernel, seq=seq, part_blocks=nh * dk // width, slab=128, qscale=dk ** -0.5)
    return pl.pallas_call(
        kern,
        grid=(batch, ncol),
        in_specs=[pl.BlockSpec((seq, width), lambda b, c: (b, col0 // width + c)),
                  pl.BlockSpec((conv_w.shape[0], width), lambda b, c: (0, c))],
        out_specs=pl.BlockSpec((seq, width), lambda b, c: (b, c)),
        out_shape=jax.ShapeDtypeStruct((batch * seq, conv_w.shape[1]), BF16),
        scratch_shapes=[pltpu.VMEM((seq + 8, width), F32)],
        compiler_params=_cparams("parallel", "parallel"),
        name="gdn_conv_norm",
    )(big, conv_w)


def _split3(x):
    p1 = x.astype(BF16)
    r1 = x - p1.astype(F32)
    p2 = r1.astype(BF16)
    p3 = (r1 - p2.astype(F32)).astype(BF16)
    return p1, p2, p3


def _inverse_level_masks(n):
    row = np.arange(n)[:, None]
    col = np.arange(n)[None, :]
    masks = [((row >> 3) == (col >> 3)) & (col < row)]
    shift = 4
    while (1 << (shift - 1)) < n:
        masks.append(((row >> shift) == (col >> shift)) & ((row >> (shift - 1)) != (col >> (shift - 1))) & (col < row))
        shift += 1
    return np.stack(masks).astype(np.float32)


def _unit_lower_inverse(a_list, mask_ref, eye):
    ab = [a.astype(BF16) for a in a_list]
    pb = [-(x * mask_ref[0]) for x in ab]
    p2b = [_dot(x, x).astype(BF16) for x in pb]
    t = [eye + x.astype(F32) for x in pb]
    t = [ti + _dot(ti.astype(BF16), x) for ti, x in zip(t, p2b)]
    p4b = [_dot(x, x).astype(BF16) for x in p2b]
    tb = [(ti + _dot(ti.astype(BF16), x)).astype(BF16) for ti, x in zip(t, p4b)]
    for lvl in range(1, mask_ref.shape[0]):
        xb = [_dot(x * mask_ref[lvl], ti).astype(BF16) for x, ti in zip(ab, tb)]
        tb = [ti - _dot(ti, x).astype(BF16) for ti, x in zip(tb, xb)]
    return tb


def _gdn_kernel(q_ref, k_ref, v_ref, z_ref, gt_ref, alog_ref, dtb_ref, nw_ref, mask_ref, o_ref, state_ref, *, hb, chunk):
    L = chunk
    heads = range(hb)

    @pl.when(pl.program_id(2) == 0)
    def _():
        state_ref[...] = jnp.zeros_like(state_ref)

    row = lax.broadcasted_iota(jnp.int32, (L, L), 0)
    col = lax.broadcasted_iota(jnp.int32, (L, L), 1)
    incl = col <= row
    strict = col < row
    tril = jnp.where(incl, 1.0, 0.0).astype(BF16)
    eye = jnp.where(row == col, 1.0, 0.0)
    lane = lax.broadcasted_iota(jnp.int32, (L, LANES), 1)

    gl = gt_ref[...]
    xg = gl + dtb_ref[0]
    softplus = jnp.maximum(xg, 0.0) + jnp.log1p(jnp.exp(-jnp.abs(xg)))
    g_all = -jnp.exp(alog_ref[0]) * softplus
    beta_all = _sigmoid(gl)
    g1, g2, g3 = _split3(g_all)
    gc_all = _dot(tril, g1) + _dot(tril, g2) + _dot(tril, g3)
    c1, c2, c3 = _split3(gc_all)
    packed = jnp.where(lane < hb, c1.astype(F32),
                       jnp.where(lane < 2 * hb, pltpu.roll(c2.astype(F32), hb, 1),
                                 pltpu.roll(c3.astype(F32), 2 * hb, 1))).astype(BF16)

    cols = [slice(i * LANES, (i + 1) * LANES) for i in heads]
    gcol = [gc_all[:, i:i + 1] for i in heads]
    bcol = [beta_all[:, hb + i:hb + i + 1] for i in heads]
    pick = [jnp.where((lane == i) | (lane == hb + i) | (lane == 2 * hb + i), 1.0, 0.0).astype(BF16) for i in heads]
    grow = [_dot_nt(pick[i], packed) for i in heads]
    decay = [jnp.exp(jnp.where(incl, gcol[i] - grow[i], -jnp.inf)) for i in heads]
    q = [q_ref[:, cols[i]] for i in heads]
    k = [k_ref[:, cols[i]] for i in heads]
    kf = [k[i].astype(F32) for i in heads]
    kbeta = [kf[i] * bcol[i] for i in heads]
    kq = [_dot_nt(jnp.concatenate([kbeta[i].astype(BF16), q[i]], axis=0), k[i]) for i in heads]
    a_kk = [jnp.where(strict, kq[i][:L] * decay[i], 0.0) for i in heads]
    a_qk = [jnp.where(incl, kq[i][L:] * decay[i], 0.0).astype(BF16) for i in heads]
    tb = _unit_lower_inverse(a_kk, mask_ref, eye)
    eg = [jnp.exp(gcol[i]) for i in heads]
    rhs = [jnp.concatenate([(v_ref[:, cols[i]].astype(F32) * bcol[i]).astype(BF16),
                            (kbeta[i] * eg[i]).astype(BF16)], axis=1) for i in heads]
    uw = [_dot(tb[i], rhs[i]) for i in heads]
    state = [state_ref[i] for i in heads]
    ws = [_dot(jnp.concatenate([uw[i][:, LANES:].astype(BF16), (q[i].astype(F32) * eg[i]).astype(BF16)], axis=0),
               state[i].astype(BF16)) for i in heads]
    vnb = [(uw[i][:, :LANES] - ws[i][:L]).astype(BF16) for i in heads]
    o = [ws[i][L:] + _dot(a_qk[i], vnb[i]) for i in heads]
    glast = [gc_all[L - 1:L, i:i + 1] for i in heads]
    k_tail = [(kf[i] * jnp.exp(glast[i] - gcol[i])).astype(BF16) for i in heads]
    new_state = [state[i] * jnp.exp(glast[i]) + _dot_tn(k_tail[i], vnb[i]) for i in heads]
    nw = nw_ref[...]
    for i in heads:
        state_ref[i] = new_state[i]
        zf = z_ref[:, cols[i]].astype(F32)
        o_ref[:, cols[i]] = (_rms(o[i], nw) * (zf * _sigmoid(zf))).astype(o_ref.dtype)


def _gdn(gqkv, big, zcol0, lat, gcol0, alog_tab, dtb_tab, norm_w, batch, seq):
    nh, hb = GDN_HEADS, GDN_HEADS_PER_STEP
    L = min(GDN_CHUNK, seq)
    assert seq % L == 0 and nh % hb == 0 and L % 8 == 0
    ng = nh // hb
    ns = seq // L
    w = hb * LANES
    masks = jnp.asarray(_inverse_level_masks(L), BF16)
    kern = functools.partial(_gdn_kernel, hb=hb, chunk=L)
    head_block = lambda part: pl.BlockSpec((L, w), lambda b, g, s: (b * ns + s, part * ng + g))
    return pl.pallas_call(
        kern,
        grid=(batch, ng, ns),
        in_specs=[head_block(0), head_block(1), head_block(2),
                  pl.BlockSpec((L, w), lambda b, g, s: (b * ns + s, zcol0 // w + g)),
                  pl.BlockSpec((L, LANES), lambda b, g, s: (b * ns + s, gcol0 // LANES + g)),
                  pl.BlockSpec((1, 1, LANES), lambda b, g, s: (g, 0, 0)),
                  pl.BlockSpec((1, 1, LANES), lambda b, g, s: (g, 0, 0)),
                  pl.BlockSpec((1, LANES), lambda b, g, s: (0, 0)),
                  pl.BlockSpec(masks.shape, lambda b, g, s: (0, 0, 0))],
        out_specs=pl.BlockSpec((L, w), lambda b, g, s: (b * ns + s, g)),
        out_shape=jax.ShapeDtypeStruct((batch * seq, nh * GDN_DV), BF16),
        scratch_shapes=[pltpu.VMEM((hb, GDN_DK, GDN_DV), F32)],
        compiler_params=_cparams("arbitrary", "arbitrary", "arbitrary"),
        name="gated_delta_net",
    )(gqkv, gqkv, gqkv, big, lat, alog_tab, dtb_tab, norm_w.reshape(1, LANES), masks)


def _mla_prep_kernel(ckv_ref, kr_ref, krs_ref, cq_ref, pos_ref, freq_ref, sign_ref, gq_ref, gkv_ref,
                     wqn_ref, wqp_ref, wqs_ref, wkn_ref, wv_ref,
                     qn_ref, qp_ref, kn_ref, kp_ref, v_ref, *, heads):
    ang = pos_ref[...].astype(F32) * freq_ref[...]
    cos = jnp.cos(ang)
    sin = jnp.sin(ang) * sign_ref[...]
    cqn = _rms(cq_ref[...], gq_ref[...]).astype(BF16)
    qn_ref[...] = _dot(cqn, wqn_ref[...]).astype(qn_ref.dtype)
    y = _dot(cqn, wqp_ref[...])
    ys = _dot(cqn, wqs_ref[...])
    for hh in range(heads):
        cols = slice(hh * LANES, (hh + 1) * LANES)
        qp_ref[:, cols] = (y[:, cols] * cos + ys[:, cols] * sin).astype(qp_ref.dtype)
    ckvn = _rms(ckv_ref[...], gkv_ref[...]).astype(BF16)
    kn_ref[...] = _dot(ckvn, wkn_ref[...]).astype(kn_ref.dtype)
    v_ref[...] = _dot(ckvn, wv_ref[...]).astype(v_ref.dtype)
    kp_ref[...] = (kr_ref[...] * cos + krs_ref[...] * sin).astype(kp_ref.dtype)


def _mla_prep(lat, pos_col2, freq, sign, gq, gkv, wqn, wqp, wqs, wkn, wv, tm=512):
    m = lat.shape[0]
    nh = MLA_HEADS
    width = nh * LANES
    const = lambda shape: pl.BlockSpec(shape, lambda i: (0, 0))
    out = pl.BlockSpec((tm, width), lambda i: (i, 0))
    qr, kvr = MLA_Q_RANK, MLA_KV_RANK
    return pl.pallas_call(
        functools.partial(_mla_prep_kernel, heads=nh),
        grid=(m // tm,),
        in_specs=[pl.BlockSpec((tm, kvr), lambda i: (i, 0)),
                  pl.BlockSpec((tm, LANES), lambda i: (i, kvr // LANES)),
                  pl.BlockSpec((tm, LANES), lambda i: (i, kvr // LANES + 1)),
                  pl.BlockSpec((tm, qr), lambda i: (i, 1)),
                  pl.BlockSpec((tm, 1), lambda i: (i, 0)),
                  const((1, LANES)), const((1, LANES)), const((1, qr)), const((1, kvr)),
                  const((qr, width)), const((qr, width)), const((qr, width)),
                  const((kvr, width)), const((kvr, width))],
        out_specs=[out, out, out, pl.BlockSpec((tm, LANES), lambda i: (i, 0)), out],
        out_shape=[jax.ShapeDtypeStruct((m, width), BF16), jax.ShapeDtypeStruct((m, width), BF16),
                   jax.ShapeDtypeStruct((m, width), BF16), jax.ShapeDtypeStruct((m, LANES), BF16),
                   jax.ShapeDtypeStruct((m, width), BF16)],
        compiler_params=_cparams("parallel"),
        name="mla_up_rope",
    )(lat, lat, lat, lat, pos_col2, freq, sign, gq.reshape(1, qr), gkv.reshape(1, kvr), wqn, wqp, wqs, wkn, wv)


def _mla_attn_kernel(qn_ref, qp_ref, kn_ref, kp_ref, v_ref, o_ref, m_ref, acc_ref, *, t, n, scale):
    tile = lambda i: slice(i * t, (i + 1) * t)
    ones = jnp.ones((t, LANES), BF16)

    def score(i, j):
        q = jnp.concatenate([qn_ref[tile(i), :], qp_ref[tile(i), :]], axis=1)
        k = jnp.concatenate([kn_ref[tile(j), :], kp_ref[tile(j), :]], axis=1)
        return _dot_nt(q, k) * scale

    value = lambda j: jnp.concatenate([v_ref[tile(j), :], ones], axis=1)
    _flash_block_causal(n, t, score, value, m_ref, acc_ref)
    _flash_write(n, t, acc_ref, o_ref)


def _mla_attention(qn, qp, kn, kp, v, batch, seq):
    nh, t = MLA_HEADS, min(ATTN_TILE, seq)
    assert seq % t == 0
    n = seq // t
    kern = functools.partial(_mla_attn_kernel, t=t, n=n, scale=(MLA_NOPE + MLA_ROPE) ** -0.5)
    head = pl.BlockSpec((seq, LANES), lambda b, h: (b, h))
    return pl.pallas_call(
        kern,
        grid=(batch, nh),
        in_specs=[head, head, head, pl.BlockSpec((seq, LANES), lambda b, h: (b, 0)), head],
        out_specs=head,
        out_shape=jax.ShapeDtypeStruct((batch * seq, nh * MLA_V), BF16),
        scratch_shapes=[pltpu.VMEM((n, t, 1), F32), pltpu.VMEM((n, t, 2 * LANES), F32)],
        compiler_params=_cparams("parallel", "parallel"),
        name="mla_attention",
    )(qn, qp, kn, kp, v)


def _pad_cols(w, width):
    return jnp.pad(w, ((0, 0), (0, width - w.shape[1])))


def _swap_halves(w):
    half = w.shape[-1] // 2
    return jnp.concatenate([w[..., half:], w[..., :half]], axis=-1)


def _layer_params(i, w_in, gdn_a_log, gdn_dt_bias, mla_w_uq, mla_w_ukv):
    moba_w = MOBA_HEADS * HEAD_DIM
    gk, gv = GDN_HEADS * GDN_DK, GDN_HEADS * GDN_DV
    sizes = (moba_w, moba_w, moba_w, gk, gk, gv, GDN_HEADS, GDN_HEADS, gv, MLA_Q_RANK, MLA_KV_RANK, MLA_ROPE)
    off = np.concatenate([[0], np.cumsum(sizes)]).tolist()
    w = w_in[i]
    seg = lambda a: w[:, off[a]:off[a + 1]]
    w_z = seg(8)
    hb = GDN_HEADS_PER_STEP
    gate_blocks = []
    for g in range(GDN_HEADS // hb):
        gate_blocks.append(_pad_cols(jnp.concatenate(
            [seg(6)[:, g * hb:(g + 1) * hb], seg(7)[:, g * hb:(g + 1) * hb]], axis=1), LANES))
    w_lat = jnp.concatenate([seg(10), _pad_cols(seg(11), LANES), _pad_cols(_swap_halves(seg(11)), LANES), seg(9)]
                            + gate_blocks, axis=1)
    alog_tab = jnp.pad(gdn_a_log[i].reshape(GDN_HEADS // hb, 1, hb), ((0, 0), (0, 0), (0, LANES - hb)))
    dtb_tab = jnp.pad(gdn_dt_bias[i].reshape(GDN_HEADS // hb, 1, hb), ((0, 0), (0, 0), (0, LANES - hb)))
    nh = MLA_HEADS
    wq = mla_w_uq[i].reshape(MLA_Q_RANK, nh, MLA_NOPE + MLA_ROPE)
    wqn = wq[:, :, :MLA_NOPE].reshape(MLA_Q_RANK, nh * MLA_NOPE).astype(BF16)
    pe = wq[:, :, MLA_NOPE:]
    padpe = lambda t: jnp.pad(t, ((0, 0), (0, 0), (0, LANES - MLA_ROPE))).reshape(MLA_Q_RANK, nh * LANES).astype(BF16)
    wkv = mla_w_ukv[i].reshape(MLA_KV_RANK, nh, MLA_NOPE + MLA_V)
    wkn = wkv[:, :, :MLA_NOPE].reshape(MLA_KV_RANK, nh * MLA_NOPE).astype(BF16)
    wv = wkv[:, :, MLA_NOPE:].reshape(MLA_KV_RANK, nh * MLA_V).astype(BF16)
    return dict(w_z=w_z, w_lat=w_lat, alog_tab=alog_tab, dtb_tab=dtb_tab,
                wqn=wqn, wqp=padpe(pe), wqs=padpe(_swap_halves(pe)), wkn=wkn, wv=wv)


def _rope_tables():
    half = MLA_ROPE // 2
    inv_freq = 1.0 / (ROPE_THETA ** (jnp.arange(half, dtype=F32) / half))
    freq = jnp.concatenate([inv_freq, inv_freq, jnp.zeros((LANES - MLA_ROPE,), F32)]).reshape(1, LANES)
    sign = jnp.concatenate([-jnp.ones((half,), F32), jnp.ones((half,), F32),
                            jnp.zeros((LANES - MLA_ROPE,), F32)]).reshape(1, LANES)
    return freq, sign


def kernel(x, p, positions, norm_mix_in, w_in, gdn_conv_w, gdn_a_log, gdn_dt_bias, gdn_norm_w, mla_q_norm_w, mla_w_uq, mla_kv_norm_w, mla_w_ukv, w_branch_gate, w_branch_a, w_branch_b, w_branch_c, w_out, norm_mix_out, norm_ffn_in, w_ffn_gate, w_ffn_up, w_ffn_down, norm_ffn_out, w_ple_gate, w_ple_proj):
    batch, seq, d = x.shape
    m = batch * seq
    depth = w_in.shape[0]
    moba_w = MOBA_HEADS * HEAD_DIM
    gdn_qkv_w = 2 * GDN_HEADS * GDN_DK + GDN_HEADS * GDN_DV
    wide = 3 * moba_w + gdn_qkv_w
    gcol0 = MLA_KV_RANK + 2 * LANES + MLA_Q_RANK
    slopes = 2.0 ** (-8.0 * jnp.arange(1, MOBA_HEADS + 1, dtype=F32) / MOBA_HEADS)
    pos_col = positions.reshape(batch, seq, 1)
    pos_row = positions.reshape(batch, 1, seq)
    pos_col2 = positions.reshape(m, 1)
    freq, sign = _rope_tables()

    xf = x.reshape(m, d)
    h = _rmsnorm(xf, norm_mix_in[0])
    for i in range(depth):
        lp = _layer_params(i, w_in, gdn_a_log, gdn_dt_bias, mla_w_uq, mla_w_ukv)
        big = _matmul(h, w_in, i, wide, BF16, 1024, 512, "in_proj_wide")
        zg = _matmul(h, lp["w_z"], 0, lp["w_z"].shape[1], BF16, 1024, 512, "in_proj_gate")
        lat = _matmul(h, lp["w_lat"], 0, lp["w_lat"].shape[1], F32, 1024, 512, "in_proj_latent")
        y_a = _moba(big, pos_col, pos_row, slopes, batch, seq)
        gqkv = _gdn_prep(big, gdn_conv_w[i], 3 * moba_w, batch, seq)
        y_b = _gdn(gqkv, zg, 0, lat, gcol0, lp["alog_tab"], lp["dtb_tab"], gdn_norm_w[i], batch, seq)
        qn, qp, kn, kp, v = _mla_prep(lat, pos_col2, freq, sign, mla_q_norm_w[i], mla_kv_norm_w[i],
                                      lp["wqn"], lp["wqp"], lp["wqs"], lp["wkn"], lp["wv"])
        y_c = _mla_attention(qn, qp, kn, kp, v, batch, seq)
        merged = _gated_merge(h, w_branch_gate, y_a, y_b, y_c, w_branch_a, w_branch_b, w_branch_c, i)
        mixed = _matmul(merged, w_out, i, d, BF16, 1024, 512, "out_proj")
        xf, h = _residual_rmsnorm(xf, mixed, norm_mix_out[i], norm_ffn_in[i], True)
        act = _ffn_up(h, w_ffn_gate, w_ffn_up, i)
        dff = act.shape[1]
        f = _matmul_ksplit(act, w_ffn_down[i].astype(BF16), BF16, 512, 1024, dff // 2, "ffn_down")
        xf, xb = _residual_rmsnorm(xf, f, norm_ffn_out[i], norm_ffn_out[i], False)
        xf = _ple(xb, xf, p[i].reshape(m, -1).astype(BF16), w_ple_gate, w_ple_proj, i)
        if i + 1 < depth:
            h = _rmsnorm(xf, norm_mix_in[i + 1])
    return xf.reshape(batch, seq, d)
```

```python
import functools

import numpy as np
import jax
import jax.numpy as jnp
from jax import lax
from jax.experimental import pallas as pl
from jax.experimental.pallas import tpu as pltpu

F32 = jnp.float32
BF16 = jnp.bfloat16

HEAD_DIM = 128
MOBA_HEADS = 8
MOBA_BLOCK = 256
MOBA_TOPK = 3
GDN_HEADS = 16
GDN_DK = 128
GDN_DV = 128
GDN_CONV = 4
GDN_CHUNK = 256
MLA_HEADS = 8
MLA_Q_RANK = 768
MLA_KV_RANK = 512
MLA_NOPE = 128
MLA_ROPE = 64
MLA_V = 128
ROPE_THETA = 10000.0
NORM_EPS = 1e-6

LANES = 128
VMEM_LIMIT_BYTES = 52 * 2 ** 20

GDN_HEADS_PER_STEP = 8
ATTN_TILE = 1024
ATTN_GROUP = 8


def _cparams(*sem):
    return pltpu.CompilerParams(dimension_semantics=sem, vmem_limit_bytes=VMEM_LIMIT_BYTES)


def _dot(a, b):
    return jnp.dot(a, b, preferred_element_type=F32)


def _dot_nt(a, b):
    return lax.dot_general(a, b, (((1,), (1,)), ((), ())), preferred_element_type=F32)


def _dot_tn(a, b):
    return lax.dot_general(a, b, (((0,), (0,)), ((), ())), preferred_element_type=F32)


def _sigmoid(x):
    return jax.nn.sigmoid(x)


def _rms(x, gain):
    return x * lax.rsqrt(jnp.mean(x * x, axis=-1, keepdims=True) + NORM_EPS) * gain


def _rms_kernel(x_ref, g_ref, o_ref):
    o_ref[...] = _rms(x_ref[...], g_ref[...]).astype(o_ref.dtype)


def _rmsnorm(x, gain, tm=256):
    m, d = x.shape
    assert m % tm == 0, (m, tm)
    return pl.pallas_call(
        _rms_kernel,
        grid=(m // tm,),
        in_specs=[pl.BlockSpec((tm, d), lambda i: (i, 0)), pl.BlockSpec((1, d), lambda i: (0, 0))],
        out_specs=pl.BlockSpec((tm, d), lambda i: (i, 0)),
        out_shape=jax.ShapeDtypeStruct((m, d), BF16),
        compiler_params=_cparams("parallel"),
        name="rmsnorm",
    )(x, gain.reshape(1, d))


def _res_rms_kernel(x_ref, y_ref, g_ref, g2_ref, xo_ref, ho_ref, *, norm_next):
    xn = x_ref[...] + _rms(y_ref[...].astype(F32), g_ref[...])
    xo_ref[...] = xn
    if norm_next:
        ho_ref[...] = _rms(xn, g2_ref[...]).astype(ho_ref.dtype)
    else:
        ho_ref[...] = xn.astype(ho_ref.dtype)


def _residual_rmsnorm(x, y, gain, gain_next, norm_next, tm=256):
    m, d = x.shape
    assert m % tm == 0, (m, tm)
    row = pl.BlockSpec((tm, d), lambda i: (i, 0))
    vec = pl.BlockSpec((1, d), lambda i: (0, 0))
    return pl.pallas_call(
        functools.partial(_res_rms_kernel, norm_next=norm_next),
        grid=(m // tm,),
        in_specs=[row, row, vec, vec],
        out_specs=[row, row],
        out_shape=[jax.ShapeDtypeStruct((m, d), F32), jax.ShapeDtypeStruct((m, d), BF16)],
        compiler_params=_cparams("parallel"),
        name="residual_rmsnorm",
    )(x, y, gain.reshape(1, d), gain_next.reshape(1, d))


def _wspec(rows, tn, layer, col=lambda j: j, single_buffer=False):
    mode = dict(pipeline_mode=pl.Buffered(1)) if single_buffer else {}
    return pl.BlockSpec((None, rows, tn), lambda j, i: (layer, 0, col(j)), **mode)


def _stacked(w):
    return w if w.ndim == 3 else w.reshape((1,) + w.shape)


def _cast_weights(pairs):
    @pl.when(pl.program_id(1) == 0)
    def _():
        for src, dst in pairs:
            dst[...] = src[...].astype(dst.dtype)


def _mm_kernel(a_ref, w_ref, o_ref, wb_ref):
    _cast_weights([(w_ref, wb_ref)])
    o_ref[...] = _dot(a_ref[...], wb_ref[...]).astype(o_ref.dtype)


def _matmul(a, w, layer, n, out_dtype, tm, tn, name, single_buffer=False):
    m, k = a.shape
    assert m % tm == 0 and n % tn == 0, (m, tm, n, tn)
    return pl.pallas_call(
        _mm_kernel,
        grid=(n // tn, m // tm),
        in_specs=[pl.BlockSpec((tm, k), lambda j, i: (i, 0)), _wspec(k, tn, layer, single_buffer=single_buffer)],
        out_specs=pl.BlockSpec((tm, tn), lambda j, i: (i, j)),
        out_shape=jax.ShapeDtypeStruct((m, n), out_dtype),
        scratch_shapes=[pltpu.VMEM((k, tn), BF16)],
        compiler_params=_cparams("parallel", "arbitrary"),
        name=name,
    )(a, _stacked(w))


def _merge_kernel(h_ref, ya_ref, yb_ref, yc_ref, wga_ref, wgb_ref, wgc_ref, wa_ref, wb_ref, wc_ref, o_ref,
                  ga_s, gb_s, gc_s, a_s, b_s, c_s):
    _cast_weights([(wga_ref, ga_s), (wgb_ref, gb_s), (wgc_ref, gc_s), (wa_ref, a_s), (wb_ref, b_s), (wc_ref, c_s)])
    h = h_ref[...]
    out = _sigmoid(_dot(h, ga_s[...])) * _dot(ya_ref[...], a_s[...])
    out += _sigmoid(_dot(h, gb_s[...])) * _dot(yb_ref[...], b_s[...])
    out += _sigmoid(_dot(h, gc_s[...])) * _dot(yc_ref[...], c_s[...])
    o_ref[...] = out.astype(o_ref.dtype)


def _gated_merge(h, w_gate, ya, yb, yc, wa, wb, wc, layer, tm=512, tn=256):
    m, d = h.shape
    assert m % tm == 0 and d % tn == 0, (m, tm, d, tn)
    nj = d // tn
    res = lambda width: pl.BlockSpec((tm, width), lambda j, i: (i, 0))
    gate = lambda br: _wspec(d, tn, layer, lambda j: br * nj + j, single_buffer=True)
    branch = lambda width: _wspec(width, tn, layer, single_buffer=True)
    widths = (wa.shape[1], wb.shape[1], wc.shape[1])
    return pl.pallas_call(
        _merge_kernel,
        grid=(nj, m // tm),
        in_specs=[res(d), res(widths[0]), res(widths[1]), res(widths[2]), gate(0), gate(1), gate(2),
                  branch(widths[0]), branch(widths[1]), branch(widths[2])],
        out_specs=pl.BlockSpec((tm, tn), lambda j, i: (i, j)),
        out_shape=jax.ShapeDtypeStruct((m, d), BF16),
        scratch_shapes=[pltpu.VMEM((d, tn), BF16)] * 3 + [pltpu.VMEM((w, tn), BF16) for w in widths],
        compiler_params=_cparams("parallel", "arbitrary"),
        name="gated_merge",
    )(h, ya, yb, yc, w_gate, w_gate, w_gate, wa, wb, wc)


def _ffn_up_kernel(h_ref, wg_ref, wu_ref, o_ref, g_s, u_s):
    _cast_weights([(wg_ref, g_s), (wu_ref, u_s)])
    h = h_ref[...]
    g = _dot(h, g_s[...])
    o_ref[...] = (g * _sigmoid(g) * _dot(h, u_s[...])).astype(o_ref.dtype)


def _ffn_up(h, wg, wu, layer, tm=1024, tn=256):
    m, d = h.shape
    n = wg.shape[2]
    assert m % tm == 0 and n % tn == 0, (m, tm, n, tn)
    return pl.pallas_call(
        _ffn_up_kernel,
        grid=(n // tn, m // tm),
        in_specs=[pl.BlockSpec((tm, d), lambda j, i: (i, 0)), _wspec(d, tn, layer), _wspec(d, tn, layer)],
        out_specs=pl.BlockSpec((tm, tn), lambda j, i: (i, j)),
        out_shape=jax.ShapeDtypeStruct((m, n), BF16),
        scratch_shapes=[pltpu.VMEM((d, tn), BF16)] * 2,
        compiler_params=_cparams("parallel", "arbitrary"),
        name="ffn_up",
    )(h, wg, wu)


def _ple_kernel(xb_ref, x_ref, p_ref, wg_ref, wp_ref, o_ref, g_s, p_s):
    _cast_weights([(wg_ref, g_s), (wp_ref, p_s)])
    gate = _sigmoid(_dot(xb_ref[...], g_s[...]))
    o_ref[...] = x_ref[...] + gate * _dot(p_ref[...], p_s[...])


def _ple(xb, x, p, wg, wp, layer, tm=1024, tn=512):
    m, d = x.shape
    dp = p.shape[1]
    assert m % tm == 0 and d % tn == 0, (m, tm, d, tn)
    return pl.pallas_call(
        _ple_kernel,
        grid=(d // tn, m // tm),
        in_specs=[pl.BlockSpec((tm, d), lambda j, i: (i, 0)),
                  pl.BlockSpec((tm, tn), lambda j, i: (i, j)),
                  pl.BlockSpec((tm, dp), lambda j, i: (i, 0)),
                  _wspec(d, tn, layer), _wspec(dp, tn, layer)],
        out_specs=pl.BlockSpec((tm, tn), lambda j, i: (i, j)),
        out_shape=jax.ShapeDtypeStruct((m, d), F32),
        scratch_shapes=[pltpu.VMEM((d, tn), BF16), pltpu.VMEM((dp, tn), BF16)],
        compiler_params=_cparams("parallel", "arbitrary"),
        name="ple",
    )(xb, x, p, wg, wp)


def _flash_block_causal(n, t, score, value, m_ref, acc_ref, row_mask=None):
    row = lax.broadcasted_iota(jnp.int32, (t, t), 0)
    col = lax.broadcasted_iota(jnp.int32, (t, t), 1)
    causal = col <= row
    for g0 in range(0, n, ATTN_GROUP):
        tiles = range(g0, min(n, g0 + ATTN_GROUP))
        s = [jnp.where(causal, score(i, i), -jnp.inf) for i in tiles]
        m = [jnp.max(x, axis=-1, keepdims=True) for x in s]
        p = [jnp.exp(x - mm).astype(BF16) for x, mm in zip(s, m)]
        for i, mm, pp in zip(tiles, m, p):
            m_ref[i] = mm
            acc_ref[i] = _dot(pp, value(i))
    for j in range(n - 1):
        for g0 in range(j + 1, n, ATTN_GROUP):
            tiles = range(g0, min(n, g0 + ATTN_GROUP))
            s = [score(i, j) for i in tiles]
            if row_mask is not None:
                s = [x if row_mask(i, j) is None else jnp.where(row_mask(i, j), x, -jnp.inf) for i, x in zip(tiles, s)]
            m_old = [m_ref[i] for i in tiles]
            m_new = [jnp.maximum(mo, jnp.max(x, axis=-1, keepdims=True)) for mo, x in zip(m_old, s)]
            p = [jnp.exp(x - mn).astype(BF16) for x, mn in zip(s, m_new)]
            vj = value(j)
            for i, mo, mn, pp in zip(tiles, m_old, m_new, p):
                m_ref[i] = mn
                acc_ref[i] = jnp.exp(mo - mn) * acc_ref[i] + _dot(pp, vj)


def _flash_write(n, t, acc_ref, o_ref):
    for i in range(n):
        a = acc_ref[i]
        o_ref[i * t:(i + 1) * t, :] = (a[:, :LANES] / a[:, LANES:]).astype(o_ref.dtype)


def _moba_kernel(slope_ref, q_ref, k_ref, v_ref, pcol_ref, prow_ref, o_ref, km_ref, m_ref, acc_ref, *, blk, nb, topk, scale):
    slope = slope_ref[pl.program_id(1)]
    tile = lambda i: slice(i * blk, (i + 1) * blk)
    ones = jnp.ones((blk, LANES), BF16)

    km_ref[...] = jnp.zeros_like(km_ref)
    for j in range(nb):
        km_ref[j:j + 1, :] = jnp.mean(k_ref[tile(j), :].astype(F32), axis=0, keepdims=True)
    km = km_ref[...]
    km_hi = km.astype(BF16)
    km_lo = (km - km_hi.astype(F32)).astype(BF16)
    lane = lax.broadcasted_iota(jnp.int32, (blk, LANES), 1)

    def selection(i):
        if i <= topk:
            return [None] * i
        q = q_ref[tile(i), :]
        gate = _dot_nt(q, km_hi) + _dot_nt(q, km_lo)
        past = lane < i
        out = []
        for j in range(i):
            gj = gate[:, j:j + 1]
            beats = jnp.where(gate > gj, 1.0, jnp.where(gate == gj, jnp.where(lane < j, 1.0, 0.0), 0.0))
            rank = jnp.sum(jnp.where(past, beats, 0.0), axis=-1, keepdims=True)
            out.append(rank < topk)
        return out

    selected = [selection(i) for i in range(nb)]

    def score(i, j):
        s = _dot_nt(q_ref[tile(i), :], k_ref[tile(j), :]) * scale
        dist = jnp.abs(pcol_ref[0, tile(i), :] - prow_ref[0, :, tile(j)])
        return s - slope * dist.astype(F32)

    value = lambda j: jnp.concatenate([v_ref[tile(j), :], ones], axis=1)
    _flash_block_causal(nb, blk, score, value, m_ref, acc_ref, row_mask=lambda i, j: selected[i][j])
    _flash_write(nb, blk, acc_ref, o_ref)


def _moba(qkv, pos_col, pos_row, slopes, batch, seq):
    blk, nh, dh = MOBA_BLOCK, MOBA_HEADS, HEAD_DIM
    assert seq % blk == 0
    nb = seq // blk
    kern = functools.partial(_moba_kernel, blk=blk, nb=nb, topk=min(MOBA_TOPK, nb), scale=dh ** -0.5)
    return pl.pallas_call(
        kern,
        grid=(batch, nh),
        in_specs=[pl.BlockSpec(memory_space=pltpu.SMEM),
                  pl.BlockSpec((seq, dh), lambda b, h: (b, h)),
                  pl.BlockSpec((seq, dh), lambda b, h: (b, nh + h)),
                  pl.BlockSpec((seq, dh), lambda b, h: (b, 2 * nh + h)),
                  pl.BlockSpec((1, seq, 1), lambda b, h: (b, 0, 0)),
                  pl.BlockSpec((1, 1, seq), lambda b, h: (b, 0, 0))],
        out_specs=pl.BlockSpec((seq, dh), lambda b, h: (b, h)),
        out_shape=jax.ShapeDtypeStruct((batch * seq, nh * dh), BF16),
        scratch_shapes=[pltpu.VMEM((LANES, dh), F32), pltpu.VMEM((nb, blk, 1), F32),
                        pltpu.VMEM((nb, blk, 2 * LANES), F32)],
        compiler_params=_cparams("parallel", "parallel"),
        name="moba_attention",
    )(slopes, qkv, qkv, qkv, pos_col, pos_row)


def _gdn_prep_kernel(x_ref, w_ref, o_ref, xs_ref, *, seq, part_blocks, slab, qscale):
    c = pl.program_id(1)
    pad = 8
    width = x_ref.shape[1]
    xs_ref[0:pad, :] = jnp.zeros((pad, width), F32)
    xs_ref[pad:pad + seq, :] = x_ref[...].astype(F32)
    w = w_ref[...]
    taps = w.shape[0]
    for r in range(0, seq, slab):
        win = xs_ref[r:r + pad + slab, :]
        y = w[0:1, :] * win
        for t in range(1, taps):
            y = pltpu.roll(y, 1, 0) + w[t:t + 1, :] * win
        y = y[pad:, :]
        y = y * _sigmoid(y)
        for g in range(width // LANES):
            yg = y[:, g * LANES:(g + 1) * LANES]
            rs = lax.rsqrt(jnp.sum(yg * yg, axis=-1, keepdims=True) + NORM_EPS)
            fac = jnp.where(c < part_blocks, rs * qscale, jnp.where(c < 2 * part_blocks, rs, 1.0))
            o_ref[r:r + slab, g * LANES:(g + 1) * LANES] = (yg * fac).astype(o_ref.dtype)


def _gdn_prep(big, conv_w, col0, batch, seq, width=4 * LANES):
    nh, dk = GDN_HEADS, GDN_DK
    ncol = conv_w.shape[1] // width
    kern = functools.partial(_gdn_prep_kernel, seq=seq, part_blocks=nh * dk // width, slab=128, qscale=dk ** -0.5)
    return pl.pallas_call(
        kern,
        grid=(batch, ncol),
        in_specs=[pl.BlockSpec((seq, width), lambda b, c: (b, col0 // width + c)),
                  pl.BlockSpec((conv_w.shape[0], width), lambda b, c: (0, c))],
        out_specs=pl.BlockSpec((seq, width), lambda b, c: (b, c)),
        out_shape=jax.ShapeDtypeStruct((batch * seq, conv_w.shape[1]), BF16),
        scratch_shapes=[pltpu.VMEM((seq + 8, width), F32)],
        compiler_params=_cparams("parallel", "parallel"),
        name="gdn_conv_norm",
    )(big, conv_w)


def _split3(x):
    p1 = x.astype(BF16)
    r1 = x - p1.astype(F32)
    p2 = r1.astype(BF16)
    p3 = (r1 - p2.astype(F32)).astype(BF16)
    return p1, p2, p3


def _inverse_level_masks(n):
    row = np.arange(n)[:, None]
    col = np.arange(n)[None, :]
    masks = [((row >> 3) == (col >> 3)) & (col < row)]
    shift = 4
    while (1 << (shift - 1)) < n:
        masks.append(((row >> shift) == (col >> shift)) & ((row >> (shift - 1)) != (col >> (shift - 1))) & (col < row))
        shift += 1
    return np.stack(masks).astype(np.float32)


def _unit_lower_inverse(a_list, mask_ref, eye):
    ab = [a.astype(BF16) for a in a_list]
    pb = [-(x * mask_ref[0]) for x in ab]
    p2b = [_dot(x, x).astype(BF16) for x in pb]
    t = [eye + x.astype(F32) for x in pb]
    t = [ti + _dot(ti.astype(BF16), x) for ti, x in zip(t, p2b)]
    p4b = [_dot(x, x).astype(BF16) for x in p2b]
    tb = [(ti + _dot(ti.astype(BF16), x)).astype(BF16) for ti, x in zip(t, p4b)]
    for lvl in range(1, mask_ref.shape[0]):
        xb = [_dot(x * mask_ref[lvl], ti).astype(BF16) for x, ti in zip(ab, tb)]
        tb = [ti - _dot(ti, x).astype(BF16) for ti, x in zip(tb, xb)]
    return tb


def _gdn_kernel(q_ref, k_ref, v_ref, z_ref, gt_ref, alog_ref, dtb_ref, nw_ref, mask_ref, o_ref, state_ref, *, hb, chunk):
    L = chunk
    heads = range(hb)

    @pl.when(pl.program_id(2) == 0)
    def _():
        state_ref[...] = jnp.zeros_like(state_ref)

    row = lax.broadcasted_iota(jnp.int32, (L, L), 0)
    col = lax.broadcasted_iota(jnp.int32, (L, L), 1)
    incl = col <= row
    strict = col < row
    tril = jnp.where(incl, 1.0, 0.0).astype(BF16)
    eye = jnp.where(row == col, 1.0, 0.0)
    lane = lax.broadcasted_iota(jnp.int32, (L, LANES), 1)

    gl = gt_ref[...]
    xg = gl + dtb_ref[0]
    softplus = jnp.maximum(xg, 0.0) + jnp.log1p(jnp.exp(-jnp.abs(xg)))
    g_all = -jnp.exp(alog_ref[0]) * softplus
    beta_all = _sigmoid(gl)
    g1, g2, g3 = _split3(g_all)
    gc_all = _dot(tril, g1) + _dot(tril, g2) + _dot(tril, g3)
    c1, c2, c3 = _split3(gc_all)
    packed = jnp.where(lane < hb, c1.astype(F32),
                       jnp.where(lane < 2 * hb, pltpu.roll(c2.astype(F32), hb, 1),
                                 pltpu.roll(c3.astype(F32), 2 * hb, 1))).astype(BF16)

    cols = [slice(i * LANES, (i + 1) * LANES) for i in heads]
    gcol = [gc_all[:, i:i + 1] for i in heads]
    bcol = [beta_all[:, hb + i:hb + i + 1] for i in heads]
    pick = [jnp.where((lane == i) | (lane == hb + i) | (lane == 2 * hb + i), 1.0, 0.0).astype(BF16) for i in heads]
    grow = [_dot_nt(pick[i], packed) for i in heads]
    decay = [jnp.exp(jnp.where(incl, gcol[i] - grow[i], -jnp.inf)) for i in heads]
    q = [q_ref[:, cols[i]] for i in heads]
    k = [k_ref[:, cols[i]] for i in heads]
    kf = [k[i].astype(F32) for i in heads]
    kbeta = [kf[i] * bcol[i] for i in heads]
    kq = [_dot_nt(jnp.concatenate([kbeta[i].astype(BF16), q[i]], axis=0), k[i]) for i in heads]
    a_kk = [jnp.where(strict, kq[i][:L] * decay[i], 0.0) for i in heads]
    a_qk = [jnp.where(incl, kq[i][L:] * decay[i], 0.0).astype(BF16) for i in heads]
    tb = _unit_lower_inverse(a_kk, mask_ref, eye)
    eg = [jnp.exp(gcol[i]) for i in heads]
    rhs = [jnp.concatenate([(v_ref[:, cols[i]].astype(F32) * bcol[i]).astype(BF16),
                            (kbeta[i] * eg[i]).astype(BF16)], axis=1) for i in heads]
    uw = [_dot(tb[i], rhs[i]) for i in heads]
    state = [state_ref[i] for i in heads]
    ws = [_dot(jnp.concatenate([uw[i][:, LANES:].astype(BF16), (q[i].astype(F32) * eg[i]).astype(BF16)], axis=0),
               state[i].astype(BF16)) for i in heads]
    vnb = [(uw[i][:, :LANES] - ws[i][:L]).astype(BF16) for i in heads]
    o = [ws[i][L:] + _dot(a_qk[i], vnb[i]) for i in heads]
    glast = [gc_all[L - 1:L, i:i + 1] for i in heads]
    k_tail = [(kf[i] * jnp.exp(glast[i] - gcol[i])).astype(BF16) for i in heads]
    new_state = [state[i] * jnp.exp(glast[i]) + _dot_tn(k_tail[i], vnb[i]) for i in heads]
    nw = nw_ref[...]
    for i in heads:
        state_ref[i] = new_state[i]
        zf = z_ref[:, cols[i]].astype(F32)
        o_ref[:, cols[i]] = (_rms(o[i], nw) * (zf * _sigmoid(zf))).astype(o_ref.dtype)


def _gdn(gqkv, big, zcol0, lat, gcol0, alog_tab, dtb_tab, norm_w, batch, seq):
    nh, hb = GDN_HEADS, GDN_HEADS_PER_STEP
    L = min(GDN_CHUNK, seq)
    assert seq % L == 0 and nh % hb == 0 and L % 8 == 0
    ng = nh // hb
    ns = seq // L
    w = hb * LANES
    masks = jnp.asarray(_inverse_level_masks(L), BF16)
    kern = functools.partial(_gdn_kernel, hb=hb, chunk=L)
    head_block = lambda part: pl.BlockSpec((L, w), lambda b, g, s: (b * ns + s, part * ng + g))
    return pl.pallas_call(
        kern,
        grid=(batch, ng, ns),
        in_specs=[head_block(0), head_block(1), head_block(2),
                  pl.BlockSpec((L, w), lambda b, g, s: (b * ns + s, zcol0 // w + g)),
                  pl.BlockSpec((L, LANES), lambda b, g, s: (b * ns + s, gcol0 // LANES + g)),
                  pl.BlockSpec((1, 1, LANES), lambda b, g, s: (g, 0, 0)),
                  pl.BlockSpec((1, 1, LANES), lambda b, g, s: (g, 0, 0)),
                  pl.BlockSpec((1, LANES), lambda b, g, s: (0, 0)),
                  pl.BlockSpec(masks.shape, lambda b, g, s: (0, 0, 0))],
        out_specs=pl.BlockSpec((L, w), lambda b, g, s: (b * ns + s, g)),
        out_shape=jax.ShapeDtypeStruct((batch * seq, nh * GDN_DV), BF16),
        scratch_shapes=[pltpu.VMEM((hb, GDN_DK, GDN_DV), F32)],
        compiler_params=_cparams("arbitrary", "arbitrary", "arbitrary"),
        name="gated_delta_net",
    )(gqkv, gqkv, gqkv, big, lat, alog_tab, dtb_tab, norm_w.reshape(1, LANES), masks)


def _mla_prep_kernel(ckv_ref, kr_ref, krs_ref, cq_ref, pos_ref, freq_ref, sign_ref, gq_ref, gkv_ref,
                     wqn_ref, wqp_ref, wqs_ref, wkn_ref, wv_ref,
                     qn_ref, qp_ref, kn_ref, kp_ref, v_ref, *, heads):
    ang = pos_ref[...].astype(F32) * freq_ref[...]
    cos = jnp.cos(ang)
    sin = jnp.sin(ang) * sign_ref[...]
    cqn = _rms(cq_ref[...], gq_ref[...]).astype(BF16)
    qn_ref[...] = _dot(cqn, wqn_ref[...]).astype(qn_ref.dtype)
    y = _dot(cqn, wqp_ref[...])
    ys = _dot(cqn, wqs_ref[...])
    for hh in range(heads):
        cols = slice(hh * LANES, (hh + 1) * LANES)
        qp_ref[:, cols] = (y[:, cols] * cos + ys[:, cols] * sin).astype(qp_ref.dtype)
    ckvn = _rms(ckv_ref[...], gkv_ref[...]).astype(BF16)
    kn_ref[...] = _dot(ckvn, wkn_ref[...]).astype(kn_ref.dtype)
    v_ref[...] = _dot(ckvn, wv_ref[...]).astype(v_ref.dtype)
    kp_ref[...] = (kr_ref[...] * cos + krs_ref[...] * sin).astype(kp_ref.dtype)


def _mla_prep(lat, pos_col2, freq, sign, gq, gkv, wqn, wqp, wqs, wkn, wv, tm=512):
    m = lat.shape[0]
    assert m % tm == 0, (m, tm)
    nh = MLA_HEADS
    width = nh * LANES
    const = lambda shape: pl.BlockSpec(shape, lambda i: (0, 0))
    out = pl.BlockSpec((tm, width), lambda i: (i, 0))
    qr, kvr = MLA_Q_RANK, MLA_KV_RANK
    return pl.pallas_call(
        functools.partial(_mla_prep_kernel, heads=nh),
        grid=(m // tm,),
        in_specs=[pl.BlockSpec((tm, kvr), lambda i: (i, 0)),
                  pl.BlockSpec((tm, LANES), lambda i: (i, kvr // LANES)),
                  pl.BlockSpec((tm, LANES), lambda i: (i, kvr // LANES + 1)),
                  pl.BlockSpec((tm, qr), lambda i: (i, 1)),
                  pl.BlockSpec((tm, 1), lambda i: (i, 0)),
                  const((1, LANES)), const((1, LANES)), const((1, qr)), const((1, kvr)),
                  const((qr, width)), const((qr, width)), const((qr, width)),
                  const((kvr, width)), const((kvr, width))],
        out_specs=[out, out, out, pl.BlockSpec((tm, LANES), lambda i: (i, 0)), out],
        out_shape=[jax.ShapeDtypeStruct((m, width), BF16), jax.ShapeDtypeStruct((m, width), BF16),
                   jax.ShapeDtypeStruct((m, width), BF16), jax.ShapeDtypeStruct((m, LANES), BF16),
                   jax.ShapeDtypeStruct((m, width), BF16)],
        compiler_params=_cparams("parallel"),
        name="mla_up_rope",
    )(lat, lat, lat, lat, pos_col2, freq, sign, gq.reshape(1, qr), gkv.reshape(1, kvr), wqn, wqp, wqs, wkn, wv)


def _mla_attn_kernel(qn_ref, qp_ref, kn_ref, kp_ref, v_ref, o_ref, m_ref, acc_ref, *, t, n, scale):
    tile = lambda i: slice(i * t, (i + 1) * t)
    ones = jnp.ones((t, LANES), BF16)

    def score(i, j):
        q = jnp.concatenate([qn_ref[tile(i), :], qp_ref[tile(i), :]], axis=1)
        k = jnp.concatenate([kn_ref[tile(j), :], kp_ref[tile(j), :]], axis=1)
        return _dot_nt(q, k) * scale

    value = lambda j: jnp.concatenate([v_ref[tile(j), :], ones], axis=1)
    _flash_block_causal(n, t, score, value, m_ref, acc_ref)
    _flash_write(n, t, acc_ref, o_ref)


def _mla_attention(qn, qp, kn, kp, v, batch, seq):
    nh, t = MLA_HEADS, min(ATTN_TILE, seq)
    assert seq % t == 0
    n = seq // t
    kern = functools.partial(_mla_attn_kernel, t=t, n=n, scale=(MLA_NOPE + MLA_ROPE) ** -0.5)
    head = pl.BlockSpec((seq, LANES), lambda b, h: (b, h))
    return pl.pallas_call(
        kern,
        grid=(batch, nh),
        in_specs=[head, head, head, pl.BlockSpec((seq, LANES), lambda b, h: (b, 0)), head],
        out_specs=head,
        out_shape=jax.ShapeDtypeStruct((batch * seq, nh * MLA_V), BF16),
        scratch_shapes=[pltpu.VMEM((n, t, 1), F32), pltpu.VMEM((n, t, 2 * LANES), F32)],
        compiler_params=_cparams("parallel", "parallel"),
        name="mla_attention",
    )(qn, qp, kn, kp, v)


def _pad_cols(w, width):
    return jnp.pad(w, ((0, 0), (0, width - w.shape[1])))


def _swap_halves(w):
    half = w.shape[-1] // 2
    return jnp.concatenate([w[..., half:], w[..., :half]], axis=-1)


def _layer_params(i, w_in, gdn_a_log, gdn_dt_bias, mla_w_uq, mla_w_ukv):
    moba_w = MOBA_HEADS * HEAD_DIM
    gk, gv = GDN_HEADS * GDN_DK, GDN_HEADS * GDN_DV
    sizes = (moba_w, moba_w, moba_w, gk, gk, gv, GDN_HEADS, GDN_HEADS, gv, MLA_Q_RANK, MLA_KV_RANK, MLA_ROPE)
    off = np.concatenate([[0], np.cumsum(sizes)]).tolist()
    seg = lambda a: w_in[i, :, off[a]:off[a + 1]]
    w_z = seg(8)
    hb = GDN_HEADS_PER_STEP
    gate_blocks = []
    for g in range(GDN_HEADS // hb):
        gate_blocks.append(_pad_cols(jnp.concatenate(
            [seg(6)[:, g * hb:(g + 1) * hb], seg(7)[:, g * hb:(g + 1) * hb]], axis=1), LANES))
    w_lat = jnp.concatenate([seg(10), _pad_cols(seg(11), LANES), _pad_cols(_swap_halves(seg(11)), LANES), seg(9)]
                            + gate_blocks, axis=1)
    alog_tab = jnp.pad(gdn_a_log[i].reshape(GDN_HEADS // hb, 1, hb), ((0, 0), (0, 0), (0, LANES - hb)))
    dtb_tab = jnp.pad(gdn_dt_bias[i].reshape(GDN_HEADS // hb, 1, hb), ((0, 0), (0, 0), (0, LANES - hb)))
    nh = MLA_HEADS
    wq = mla_w_uq[i].reshape(MLA_Q_RANK, nh, MLA_NOPE + MLA_ROPE)
    wqn = wq[:, :, :MLA_NOPE].reshape(MLA_Q_RANK, nh * MLA_NOPE).astype(BF16)
    pe = wq[:, :, MLA_NOPE:]
    padpe = lambda t: jnp.pad(t, ((0, 0), (0, 0), (0, LANES - MLA_ROPE))).reshape(MLA_Q_RANK, nh * LANES).astype(BF16)
    wkv = mla_w_ukv[i].reshape(MLA_KV_RANK, nh, MLA_NOPE + MLA_V)
    wkn = wkv[:, :, :MLA_NOPE].reshape(MLA_KV_RANK, nh * MLA_NOPE).astype(BF16)
    wv = wkv[:, :, MLA_NOPE:].reshape(MLA_KV_RANK, nh * MLA_V).astype(BF16)
    return dict(w_z=w_z, w_lat=w_lat, alog_tab=alog_tab, dtb_tab=dtb_tab,
                wqn=wqn, wqp=padpe(pe), wqs=padpe(_swap_halves(pe)), wkn=wkn, wv=wv)


def _rope_tables():
    half = MLA_ROPE // 2
    inv_freq = 1.0 / (ROPE_THETA ** (jnp.arange(half, dtype=F32) / half))
    freq = jnp.concatenate([inv_freq, inv_freq, jnp.zeros((LANES - MLA_ROPE,), F32)]).reshape(1, LANES)
    sign = jnp.concatenate([-jnp.ones((half,), F32), jnp.ones((half,), F32),
                            jnp.zeros((LANES - MLA_ROPE,), F32)]).reshape(1, LANES)
    return freq, sign


def kernel(x, p, positions, norm_mix_in, w_in, gdn_conv_w, gdn_a_log, gdn_dt_bias, gdn_norm_w, mla_q_norm_w, mla_w_uq, mla_kv_norm_w, mla_w_ukv, w_branch_gate, w_branch_a, w_branch_b, w_branch_c, w_out, norm_mix_out, norm_ffn_in, w_ffn_gate, w_ffn_up, w_ffn_down, norm_ffn_out, w_ple_gate, w_ple_proj):
    batch, seq, d = x.shape
    m = batch * seq
    depth = w_in.shape[0]
    moba_w = MOBA_HEADS * HEAD_DIM
    gdn_qkv_w = 2 * GDN_HEADS * GDN_DK + GDN_HEADS * GDN_DV
    wide = 3 * moba_w + gdn_qkv_w
    gcol0 = MLA_KV_RANK + 2 * LANES + MLA_Q_RANK
    slopes = 2.0 ** (-8.0 * jnp.arange(1, MOBA_HEADS + 1, dtype=F32) / MOBA_HEADS)
    pos_col = positions.reshape(batch, seq, 1)
    pos_row = positions.reshape(batch, 1, seq)
    pos_col2 = positions.reshape(m, 1)
    freq, sign = _rope_tables()

    xf = x.reshape(m, d)
    h = _rmsnorm(xf, norm_mix_in[0])
    for i in range(depth):
        lp = _layer_params(i, w_in, gdn_a_log, gdn_dt_bias, mla_w_uq, mla_w_ukv)
        big = _matmul(h, w_in, i, wide, BF16, 1024, 512, "in_proj_wide")
        zg = _matmul(h, lp["w_z"], 0, lp["w_z"].shape[1], BF16, 1024, 512, "in_proj_gate")
        lat = _matmul(h, lp["w_lat"], 0, lp["w_lat"].shape[1], F32, 1024, 256, "in_proj_latent")
        y_a = _moba(big, pos_col, pos_row, slopes, batch, seq)
        gqkv = _gdn_prep(big, gdn_conv_w[i], 3 * moba_w, batch, seq)
        y_b = _gdn(gqkv, zg, 0, lat, gcol0, lp["alog_tab"], lp["dtb_tab"], gdn_norm_w[i], batch, seq)
        qn, qp, kn, kp, v = _mla_prep(lat, pos_col2, freq, sign, mla_q_norm_w[i], mla_kv_norm_w[i],
                                      lp["wqn"], lp["wqp"], lp["wqs"], lp["wkn"], lp["wv"])
        y_c = _mla_attention(qn, qp, kn, kp, v, batch, seq)
        merged = _gated_merge(h, w_branch_gate, y_a, y_b, y_c, w_branch_a, w_branch_b, w_branch_c, i)
        mixed = _matmul(merged, w_out, i, d, BF16, 1024, 512, "out_proj")
        xf, h = _residual_rmsnorm(xf, mixed, norm_mix_out[i], norm_ffn_in[i], True)
        act = _ffn_up(h, w_ffn_gate, w_ffn_up, i)
        f = _matmul(act, w_ffn_down, i, d, BF16, 256, 512, "ffn_down", single_buffer=True)
        xf, xb = _residual_rmsnorm(xf, f, norm_ffn_out[i], norm_ffn_out[i], False)
        xf = _ple(xb, xf, p[i].reshape(m, -1).astype(BF16), w_ple_gate, w_ple_proj, i)
        if i + 1 < depth:
            h = _rmsnorm(xf, norm_mix_in[i + 1])
    return xf.reshape(batch, seq, d)
```

```python
import functools

import numpy as np
import jax
import jax.numpy as jnp
from jax import lax
from jax.experimental import pallas as pl
from jax.experimental.pallas import tpu as pltpu

F32 = jnp.float32
BF16 = jnp.bfloat16

HEAD_DIM = 128
MOBA_HEADS = 8
MOBA_BLOCK = 256
MOBA_TOPK = 3
GDN_HEADS = 16
GDN_DK = 128
GDN_DV = 128
GDN_CONV = 4
GDN_CHUNK = 256
MLA_HEADS = 8
MLA_Q_RANK = 768
MLA_KV_RANK = 512
MLA_NOPE = 128
MLA_ROPE = 64
MLA_V = 128
ROPE_THETA = 10000.0
NORM_EPS = 1e-6

LANES = 128
VMEM_LIMIT_BYTES = 52 * 2 ** 20

PROJ_TILE = (1024, 512)
DOWN_TILE = (256, 512)
GDN_HEADS_PER_STEP = 8
ATTN_TILE = 1024
ATTN_GROUP = 8


def _cparams(*sem):
    return pltpu.CompilerParams(dimension_semantics=sem, vmem_limit_bytes=VMEM_LIMIT_BYTES)


def _dot(a, b):
    return jnp.dot(a, b, preferred_element_type=F32)


def _dot_nt(a, b):
    return lax.dot_general(a, b, (((1,), (1,)), ((), ())), preferred_element_type=F32)


def _dot_tn(a, b):
    return lax.dot_general(a, b, (((0,), (0,)), ((), ())), preferred_element_type=F32)


def _sigmoid(x):
    return jax.nn.sigmoid(x)


def _rms(x, gain):
    return x * lax.rsqrt(jnp.mean(x * x, axis=-1, keepdims=True) + NORM_EPS) * gain


def _rms_kernel(x_ref, g_ref, o_ref):
    o_ref[...] = _rms(x_ref[...], g_ref[...]).astype(o_ref.dtype)


def _rmsnorm(x, gain, tm=256):
    m, d = x.shape
    assert m % tm == 0, (m, tm)
    return pl.pallas_call(
        _rms_kernel,
        grid=(m // tm,),
        in_specs=[pl.BlockSpec((tm, d), lambda i: (i, 0)), pl.BlockSpec((1, d), lambda i: (0, 0))],
        out_specs=pl.BlockSpec((tm, d), lambda i: (i, 0)),
        out_shape=jax.ShapeDtypeStruct((m, d), BF16),
        compiler_params=_cparams("parallel"),
        name="rmsnorm",
    )(x, gain.reshape(1, d))


def _res_rms_kernel(x_ref, y_ref, g_ref, g2_ref, xo_ref, ho_ref, *, norm_next):
    xn = x_ref[...] + _rms(y_ref[...].astype(F32), g_ref[...])
    xo_ref[...] = xn
    if norm_next:
        ho_ref[...] = _rms(xn, g2_ref[...]).astype(ho_ref.dtype)
    else:
        ho_ref[...] = xn.astype(ho_ref.dtype)


def _residual_rmsnorm(x, y, gain, gain_next, norm_next, tm=256):
    m, d = x.shape
    assert m % tm == 0, (m, tm)
    row = pl.BlockSpec((tm, d), lambda i: (i, 0))
    vec = pl.BlockSpec((1, d), lambda i: (0, 0))
    return pl.pallas_call(
        functools.partial(_res_rms_kernel, norm_next=norm_next),
        grid=(m // tm,),
        in_specs=[row, row, vec, vec],
        out_specs=[row, row],
        out_shape=[jax.ShapeDtypeStruct((m, d), F32), jax.ShapeDtypeStruct((m, d), BF16)],
        compiler_params=_cparams("parallel"),
        name="residual_rmsnorm",
    )(x, y, gain.reshape(1, d), gain_next.reshape(1, d))


def _wspec(rows, tn, layer, col=lambda j: j, single_buffer=False):
    mode = dict(pipeline_mode=pl.Buffered(1)) if single_buffer else {}
    return pl.BlockSpec((None, rows, tn), lambda j, i: (layer, 0, col(j)), **mode)


def _stacked(w):
    return w if w.ndim == 3 else w.reshape((1,) + w.shape)


def _cast_weights(pairs):
    @pl.when(pl.program_id(1) == 0)
    def _():
        for src, dst in pairs:
            dst[...] = src[...].astype(dst.dtype)


def _mm_kernel(a_ref, w_ref, o_ref, *scratch):
    if scratch:
        _cast_weights([(w_ref, scratch[0])])
        w_ref = scratch[0]
    o_ref[...] = _dot(a_ref[...], w_ref[...]).astype(o_ref.dtype)


def _matmul(a, w, layer, n, out_dtype, tm, tn, name, single_buffer=False):
    m, k = a.shape
    assert m % tm == 0 and n % tn == 0, (m, tm, n, tn)
    return pl.pallas_call(
        _mm_kernel,
        grid=(n // tn, m // tm),
        in_specs=[pl.BlockSpec((tm, k), lambda j, i: (i, 0)), _wspec(k, tn, layer, single_buffer=single_buffer)],
        out_specs=pl.BlockSpec((tm, tn), lambda j, i: (i, j)),
        out_shape=jax.ShapeDtypeStruct((m, n), out_dtype),
        scratch_shapes=[] if w.dtype == BF16 else [pltpu.VMEM((k, tn), BF16)],
        compiler_params=_cparams("parallel", "arbitrary"),
        name=name,
    )(a, _stacked(w))


def _merge_kernel(h_ref, ya_ref, yb_ref, yc_ref, wga_ref, wgb_ref, wgc_ref, wa_ref, wb_ref, wc_ref, o_ref,
                  ga_s, gb_s, gc_s, a_s, b_s, c_s):
    _cast_weights([(wga_ref, ga_s), (wgb_ref, gb_s), (wgc_ref, gc_s), (wa_ref, a_s), (wb_ref, b_s), (wc_ref, c_s)])
    h = h_ref[...]
    out = _sigmoid(_dot(h, ga_s[...])) * _dot(ya_ref[...], a_s[...])
    out += _sigmoid(_dot(h, gb_s[...])) * _dot(yb_ref[...], b_s[...])
    out += _sigmoid(_dot(h, gc_s[...])) * _dot(yc_ref[...], c_s[...])
    o_ref[...] = out.astype(o_ref.dtype)


def _gated_merge(h, w_gate, ya, yb, yc, wa, wb, wc, layer, tm=512, tn=256):
    m, d = h.shape
    assert m % tm == 0 and d % tn == 0, (m, tm, d, tn)
    nj = d // tn
    res = lambda width: pl.BlockSpec((tm, width), lambda j, i: (i, 0))
    gate = lambda br: _wspec(d, tn, layer, lambda j: br * nj + j, single_buffer=True)
    branch = lambda width: _wspec(width, tn, layer, single_buffer=True)
    widths = (wa.shape[1], wb.shape[1], wc.shape[1])
    return pl.pallas_call(
        _merge_kernel,
        grid=(nj, m // tm),
        in_specs=[res(d), res(widths[0]), res(widths[1]), res(widths[2]), gate(0), gate(1), gate(2),
                  branch(widths[0]), branch(widths[1]), branch(widths[2])],
        out_specs=pl.BlockSpec((tm, tn), lambda j, i: (i, j)),
        out_shape=jax.ShapeDtypeStruct((m, d), BF16),
        scratch_shapes=[pltpu.VMEM((d, tn), BF16)] * 3 + [pltpu.VMEM((w, tn), BF16) for w in widths],
        compiler_params=_cparams("parallel", "arbitrary"),
        name="gated_merge",
    )(h, ya, yb, yc, w_gate, w_gate, w_gate, wa, wb, wc)


def _ffn_up_kernel(h_ref, wg_ref, wu_ref, o_ref, g_s, u_s):
    _cast_weights([(wg_ref, g_s), (wu_ref, u_s)])
    h = h_ref[...]
    g = _dot(h, g_s[...])
    o_ref[...] = (g * _sigmoid(g) * _dot(h, u_s[...])).astype(o_ref.dtype)


def _ffn_up(h, wg, wu, layer, tm=1024, tn=256):
    m, d = h.shape
    n = wg.shape[2]
    assert m % tm == 0 and n % tn == 0, (m, tm, n, tn)
    return pl.pallas_call(
        _ffn_up_kernel,
        grid=(n // tn, m // tm),
        in_specs=[pl.BlockSpec((tm, d), lambda j, i: (i, 0)), _wspec(d, tn, layer), _wspec(d, tn, layer)],
        out_specs=pl.BlockSpec((tm, tn), lambda j, i: (i, j)),
        out_shape=jax.ShapeDtypeStruct((m, n), BF16),
        scratch_shapes=[pltpu.VMEM((d, tn), BF16)] * 2,
        compiler_params=_cparams("parallel", "arbitrary"),
        name="ffn_up",
    )(h, wg, wu)


def _ple_kernel(xb_ref, x_ref, p_ref, wg_ref, wp_ref, o_ref, g_s, p_s):
    _cast_weights([(wg_ref, g_s), (wp_ref, p_s)])
    gate = _sigmoid(_dot(xb_ref[...], g_s[...]))
    o_ref[...] = x_ref[...] + gate * _dot(p_ref[...], p_s[...])


def _ple(xb, x, p, wg, wp, layer, tm=1024, tn=512):
    m, d = x.shape
    dp = p.shape[1]
    assert m % tm == 0 and d % tn == 0, (m, tm, d, tn)
    return pl.pallas_call(
        _ple_kernel,
        grid=(d // tn, m // tm),
        in_specs=[pl.BlockSpec((tm, d), lambda j, i: (i, 0)),
                  pl.BlockSpec((tm, tn), lambda j, i: (i, j)),
                  pl.BlockSpec((tm, dp), lambda j, i: (i, 0)),
                  _wspec(d, tn, layer), _wspec(dp, tn, layer)],
        out_specs=pl.BlockSpec((tm, tn), lambda j, i: (i, j)),
        out_shape=jax.ShapeDtypeStruct((m, d), F32),
        scratch_shapes=[pltpu.VMEM((d, tn), BF16), pltpu.VMEM((dp, tn), BF16)],
        compiler_params=_cparams("parallel", "arbitrary"),
        name="ple",
    )(xb, x, p, wg, wp)


def _flash_block_causal(n, t, score, value, m_ref, acc_ref, row_mask=None):
    row = lax.broadcasted_iota(jnp.int32, (t, t), 0)
    col = lax.broadcasted_iota(jnp.int32, (t, t), 1)
    causal = col <= row
    for g0 in range(0, n, ATTN_GROUP):
        tiles = range(g0, min(n, g0 + ATTN_GROUP))
        s = [jnp.where(causal, score(i, i), -jnp.inf) for i in tiles]
        m = [jnp.max(x, axis=-1, keepdims=True) for x in s]
        p = [jnp.exp(x - mm).astype(BF16) for x, mm in zip(s, m)]
        for i, mm, pp in zip(tiles, m, p):
            m_ref[i] = mm
            acc_ref[i] = _dot(pp, value(i))
    for j in range(n - 1):
        for g0 in range(j + 1, n, ATTN_GROUP):
            tiles = range(g0, min(n, g0 + ATTN_GROUP))
            s = [score(i, j) for i in tiles]
            if row_mask is not None:
                s = [x if row_mask(i, j) is None else jnp.where(row_mask(i, j), x, -jnp.inf) for i, x in zip(tiles, s)]
            m_old = [m_ref[i] for i in tiles]
            m_new = [jnp.maximum(mo, jnp.max(x, axis=-1, keepdims=True)) for mo, x in zip(m_old, s)]
            p = [jnp.exp(x - mn).astype(BF16) for x, mn in zip(s, m_new)]
            vj = value(j)
            for i, mo, mn, pp in zip(tiles, m_old, m_new, p):
                m_ref[i] = mn
                acc_ref[i] = jnp.exp(mo - mn) * acc_ref[i] + _dot(pp, vj)


def _flash_write(n, t, acc_ref, o_ref):
    for i in range(n):
        a = acc_ref[i]
        o_ref[i * t:(i + 1) * t, :] = (a[:, :LANES] / a[:, LANES:]).astype(o_ref.dtype)


def _moba_kernel(slope_ref, q_ref, k_ref, v_ref, pcol_ref, prow_ref, o_ref, km_ref, m_ref, acc_ref, *, blk, nb, topk, scale):
    slope = slope_ref[pl.program_id(1)]
    tile = lambda i: slice(i * blk, (i + 1) * blk)
    ones = jnp.ones((blk, LANES), BF16)

    km_ref[...] = jnp.zeros_like(km_ref)
    for j in range(nb):
        km_ref[j:j + 1, :] = jnp.mean(k_ref[tile(j), :].astype(F32), axis=0, keepdims=True)
    km = km_ref[...]
    km_hi = km.astype(BF16)
    km_lo = (km - km_hi.astype(F32)).astype(BF16)
    lane = lax.broadcasted_iota(jnp.int32, (blk, LANES), 1)

    def selection(i):
        if i <= topk:
            return [None] * i
        q = q_ref[tile(i), :]
        gate = _dot_nt(q, km_hi) + _dot_nt(q, km_lo)
        past = lane < i
        out = []
        for j in range(i):
            gj = gate[:, j:j + 1]
            beats = jnp.where(gate > gj, 1.0, jnp.where(gate == gj, jnp.where(lane < j, 1.0, 0.0), 0.0))
            rank = jnp.sum(jnp.where(past, beats, 0.0), axis=-1, keepdims=True)
            out.append(rank < topk)
        return out

    selected = [selection(i) for i in range(nb)]

    def score(i, j):
        s = _dot_nt(q_ref[tile(i), :], k_ref[tile(j), :]) * scale
        dist = jnp.abs(pcol_ref[0, tile(i), :] - prow_ref[0, :, tile(j)])
        return s - slope * dist.astype(F32)

    value = lambda j: jnp.concatenate([v_ref[tile(j), :], ones], axis=1)
    _flash_block_causal(nb, blk, score, value, m_ref, acc_ref, row_mask=lambda i, j: selected[i][j])
    _flash_write(nb, blk, acc_ref, o_ref)


def _moba(qkv, pos_col, pos_row, slopes, batch, seq):
    blk, nh, dh = MOBA_BLOCK, MOBA_HEADS, HEAD_DIM
    assert seq % blk == 0
    nb = seq // blk
    kern = functools.partial(_moba_kernel, blk=blk, nb=nb, topk=min(MOBA_TOPK, nb), scale=dh ** -0.5)
    return pl.pallas_call(
        kern,
        grid=(batch, nh),
        in_specs=[pl.BlockSpec(memory_space=pltpu.SMEM),
                  pl.BlockSpec((seq, dh), lambda b, h: (b, h)),
                  pl.BlockSpec((seq, dh), lambda b, h: (b, nh + h)),
                  pl.BlockSpec((seq, dh), lambda b, h: (b, 2 * nh + h)),
                  pl.BlockSpec((1, seq, 1), lambda b, h: (b, 0, 0)),
                  pl.BlockSpec((1, 1, seq), lambda b, h: (b, 0, 0))],
        out_specs=pl.BlockSpec((seq, dh), lambda b, h: (b, h)),
        out_shape=jax.ShapeDtypeStruct((batch * seq, nh * dh), BF16),
        scratch_shapes=[pltpu.VMEM((LANES, dh), F32), pltpu.VMEM((nb, blk, 1), F32),
                        pltpu.VMEM((nb, blk, 2 * LANES), F32)],
        compiler_params=_cparams("parallel", "parallel"),
        name="moba_attention",
    )(slopes, qkv, qkv, qkv, pos_col, pos_row)


def _split3(x):
    p1 = x.astype(BF16)
    r1 = x - p1.astype(F32)
    p2 = r1.astype(BF16)
    p3 = (r1 - p2.astype(F32)).astype(BF16)
    return p1, p2, p3


def _inverse_level_masks(n):
    row = np.arange(n)[:, None]
    col = np.arange(n)[None, :]
    masks = [((row >> 3) == (col >> 3)) & (col < row)]
    shift = 4
    while (1 << (shift - 1)) < n:
        masks.append(((row >> shift) == (col >> shift)) & ((row >> (shift - 1)) != (col >> (shift - 1))) & (col < row))
        shift += 1
    return np.stack(masks).astype(np.float32)


def _unit_lower_inverse(a_list, mask_ref, eye):
    ab = [a.astype(BF16) for a in a_list]
    pb = [-(x * mask_ref[0]) for x in ab]
    p2b = [_dot(x, x).astype(BF16) for x in pb]
    t = [eye + x.astype(F32) for x in pb]
    t = [ti + _dot(ti.astype(BF16), x) for ti, x in zip(t, p2b)]
    p4b = [_dot(x, x).astype(BF16) for x in p2b]
    tb = [(ti + _dot(ti.astype(BF16), x)).astype(BF16) for ti, x in zip(t, p4b)]
    for lvl in range(1, mask_ref.shape[0]):
        xb = [_dot(x * mask_ref[lvl], ti).astype(BF16) for x, ti in zip(ab, tb)]
        tb = [ti - _dot(ti, x).astype(BF16) for ti, x in zip(tb, xb)]
    return tb


def _gdn_kernel(q_ref, k_ref, v_ref, cwq_ref, cwk_ref, cwv_ref, z_ref, gt_ref, alog_ref, dtb_ref, nw_ref, mask_ref,
                o_ref, state_ref, halo_ref, *, hb, chunk, qscale):
    L = chunk
    heads = range(hb)
    pad = halo_ref.shape[1]

    @pl.when(pl.program_id(2) == 0)
    def _():
        state_ref[...] = jnp.zeros_like(state_ref)
        halo_ref[...] = jnp.zeros_like(halo_ref)

    def conv_silu(x_ref, cw_ref, part):
        x = x_ref[...].astype(F32)
        win = jnp.concatenate([halo_ref[part], x], axis=0)
        cw = cw_ref[...]
        y = cw[0:1, :] * win
        for t in range(1, cw.shape[0]):
            y = pltpu.roll(y, 1, 0) + cw[t:t + 1, :] * win
        halo_ref[part] = x[L - pad:, :]
        y = y[pad:, :]
        return y * _sigmoid(y)

    def l2norm(y):
        return y * lax.rsqrt(jnp.sum(y * y, axis=-1, keepdims=True) + NORM_EPS)

    row = lax.broadcasted_iota(jnp.int32, (L, L), 0)
    col = lax.broadcasted_iota(jnp.int32, (L, L), 1)
    incl = col <= row
    strict = col < row
    tril = jnp.where(incl, 1.0, 0.0).astype(BF16)
    eye = jnp.where(row == col, 1.0, 0.0)
    lane = lax.broadcasted_iota(jnp.int32, (L, LANES), 1)

    gl = gt_ref[...]
    xg = gl + dtb_ref[0]
    softplus = jnp.maximum(xg, 0.0) + jnp.log1p(jnp.exp(-jnp.abs(xg)))
    g_all = -jnp.exp(alog_ref[0]) * softplus
    beta_all = _sigmoid(gl)
    g1, g2, g3 = _split3(g_all)
    gc_all = _dot(tril, g1) + _dot(tril, g2) + _dot(tril, g3)
    c1, c2, c3 = _split3(gc_all)
    packed = jnp.where(lane < hb, c1.astype(F32),
                       jnp.where(lane < 2 * hb, pltpu.roll(c2.astype(F32), hb, 1),
                                 pltpu.roll(c3.astype(F32), 2 * hb, 1))).astype(BF16)

    cols = [slice(i * LANES, (i + 1) * LANES) for i in heads]
    gcol = [gc_all[:, i:i + 1] for i in heads]
    bcol = [beta_all[:, hb + i:hb + i + 1] for i in heads]
    pick = [jnp.where((lane == i) | (lane == hb + i) | (lane == 2 * hb + i), 1.0, 0.0).astype(BF16) for i in heads]
    grow = [_dot_nt(pick[i], packed) for i in heads]
    decay = [jnp.exp(jnp.where(incl, gcol[i] - grow[i], -jnp.inf)) for i in heads]
    yq = conv_silu(q_ref, cwq_ref, 0)
    yk = conv_silu(k_ref, cwk_ref, 1)
    yv = conv_silu(v_ref, cwv_ref, 2)
    q = [(l2norm(yq[:, cols[i]]) * qscale).astype(BF16) for i in heads]
    k = [l2norm(yk[:, cols[i]]).astype(BF16) for i in heads]
    kf = [k[i].astype(F32) for i in heads]
    kbeta = [kf[i] * bcol[i] for i in heads]
    kq = [_dot_nt(jnp.concatenate([kbeta[i].astype(BF16), q[i]], axis=0), k[i]) for i in heads]
    a_kk = [jnp.where(strict, kq[i][:L] * decay[i], 0.0) for i in heads]
    a_qk = [jnp.where(incl, kq[i][L:] * decay[i], 0.0).astype(BF16) for i in heads]
    tb = _unit_lower_inverse(a_kk, mask_ref, eye)
    eg = [jnp.exp(gcol[i]) for i in heads]
    rhs = [jnp.concatenate([(yv[:, cols[i]] * bcol[i]).astype(BF16),
                            (kbeta[i] * eg[i]).astype(BF16)], axis=1) for i in heads]
    uw = [_dot(tb[i], rhs[i]) for i in heads]
    state = [state_ref[i] for i in heads]
    ws = [_dot(jnp.concatenate([uw[i][:, LANES:].astype(BF16), (q[i].astype(F32) * eg[i]).astype(BF16)], axis=0),
               state[i].astype(BF16)) for i in heads]
    vnb = [(uw[i][:, :LANES] - ws[i][:L]).astype(BF16) for i in heads]
    o = [ws[i][L:] + _dot(a_qk[i], vnb[i]) for i in heads]
    glast = [gc_all[L - 1:L, i:i + 1] for i in heads]
    k_tail = [(kf[i] * jnp.exp(glast[i] - gcol[i])).astype(BF16) for i in heads]
    new_state = [state[i] * jnp.exp(glast[i]) + _dot_tn(k_tail[i], vnb[i]) for i in heads]
    nw = nw_ref[...]
    for i in heads:
        state_ref[i] = new_state[i]
        zf = z_ref[:, cols[i]].astype(F32)
        o_ref[:, cols[i]] = (_rms(o[i], nw) * (zf * _sigmoid(zf))).astype(o_ref.dtype)


def _gdn(qkv, qcol0, conv_w, zg, zcol0, lat, gcol0, alog_tab, dtb_tab, norm_w, batch, seq):
    nh, hb = GDN_HEADS, GDN_HEADS_PER_STEP
    L = min(GDN_CHUNK, seq)
    ng = nh // hb
    ns = seq // L
    w = hb * LANES
    pad = 8
    taps = conv_w.shape[0]
    assert seq % L == 0 and nh % hb == 0 and L % 8 == 0 and qcol0 % w == 0 and zcol0 % w == 0 and taps - 1 <= pad
    masks = jnp.asarray(_inverse_level_masks(L), BF16)
    kern = functools.partial(_gdn_kernel, hb=hb, chunk=L, qscale=GDN_DK ** -0.5)
    head_block = lambda part: pl.BlockSpec((L, w), lambda b, g, s: (b * ns + s, qcol0 // w + part * ng + g))
    conv_block = lambda part: pl.BlockSpec((taps, w), lambda b, g, s: (0, part * ng + g))
    return pl.pallas_call(
        kern,
        grid=(batch, ng, ns),
        in_specs=[head_block(0), head_block(1), head_block(2), conv_block(0), conv_block(1), conv_block(2),
                  pl.BlockSpec((L, w), lambda b, g, s: (b * ns + s, zcol0 // w + g)),
                  pl.BlockSpec((L, LANES), lambda b, g, s: (b * ns + s, gcol0 // LANES + g)),
                  pl.BlockSpec((1, 1, LANES), lambda b, g, s: (g, 0, 0)),
                  pl.BlockSpec((1, 1, LANES), lambda b, g, s: (g, 0, 0)),
                  pl.BlockSpec((1, LANES), lambda b, g, s: (0, 0)),
                  pl.BlockSpec(masks.shape, lambda b, g, s: (0, 0, 0))],
        out_specs=pl.BlockSpec((L, w), lambda b, g, s: (b * ns + s, g)),
        out_shape=jax.ShapeDtypeStruct((batch * seq, nh * GDN_DV), BF16),
        scratch_shapes=[pltpu.VMEM((hb, GDN_DK, GDN_DV), F32), pltpu.VMEM((3, pad, w), F32)],
        compiler_params=_cparams("arbitrary", "arbitrary", "arbitrary"),
        name="gated_delta_net",
    )(qkv, qkv, qkv, conv_w, conv_w, conv_w, zg, lat, alog_tab, dtb_tab, norm_w.reshape(1, LANES), masks)


def _mla_prep_kernel(ckv_ref, kr_ref, krs_ref, cq_ref, pos_ref, freq_ref, sign_ref, gq_ref, gkv_ref,
                     wqn_ref, wqp_ref, wqs_ref, wkn_ref, wv_ref,
                     qn_ref, qp_ref, kn_ref, kp_ref, v_ref, *, heads):
    ang = pos_ref[...].astype(F32) * freq_ref[...]
    cos = jnp.cos(ang)
    sin = jnp.sin(ang) * sign_ref[...]
    cqn = _rms(cq_ref[...], gq_ref[...]).astype(BF16)
    qn_ref[...] = _dot(cqn, wqn_ref[...]).astype(qn_ref.dtype)
    y = _dot(cqn, wqp_ref[...])
    ys = _dot(cqn, wqs_ref[...])
    for hh in range(heads):
        cols = slice(hh * LANES, (hh + 1) * LANES)
        qp_ref[:, cols] = (y[:, cols] * cos + ys[:, cols] * sin).astype(qp_ref.dtype)
    ckvn = _rms(ckv_ref[...], gkv_ref[...]).astype(BF16)
    kn_ref[...] = _dot(ckvn, wkn_ref[...]).astype(kn_ref.dtype)
    v_ref[...] = _dot(ckvn, wv_ref[...]).astype(v_ref.dtype)
    kp_ref[...] = (kr_ref[...] * cos + krs_ref[...] * sin).astype(kp_ref.dtype)


def _mla_prep(lat, pos_col2, freq, sign, gq, gkv, wqn, wqp, wqs, wkn, wv, tm=512):
    m = lat.shape[0]
    assert m % tm == 0, (m, tm)
    nh = MLA_HEADS
    width = nh * LANES
    const = lambda shape: pl.BlockSpec(shape, lambda i: (0, 0))
    out = pl.BlockSpec((tm, width), lambda i: (i, 0))
    qr, kvr = MLA_Q_RANK, MLA_KV_RANK
    return pl.pallas_call(
        functools.partial(_mla_prep_kernel, heads=nh),
        grid=(m // tm,),
        in_specs=[pl.BlockSpec((tm, kvr), lambda i: (i, 0)),
                  pl.BlockSpec((tm, LANES), lambda i: (i, kvr // LANES)),
                  pl.BlockSpec((tm, LANES), lambda i: (i, kvr // LANES + 1)),
                  pl.BlockSpec((tm, qr), lambda i: (i, 1)),
                  pl.BlockSpec((tm, 1), lambda i: (i, 0)),
                  const((1, LANES)), const((1, LANES)), const((1, qr)), const((1, kvr)),
                  const((qr, width)), const((qr, width)), const((qr, width)),
                  const((kvr, width)), const((kvr, width))],
        out_specs=[out, out, out, pl.BlockSpec((tm, LANES), lambda i: (i, 0)), out],
        out_shape=[jax.ShapeDtypeStruct((m, width), BF16), jax.ShapeDtypeStruct((m, width), BF16),
                   jax.ShapeDtypeStruct((m, width), BF16), jax.ShapeDtypeStruct((m, LANES), BF16),
                   jax.ShapeDtypeStruct((m, width), BF16)],
        compiler_params=_cparams("parallel"),
        name="mla_up_rope",
    )(lat, lat, lat, lat, pos_col2, freq, sign, gq.reshape(1, qr), gkv.reshape(1, kvr), wqn, wqp, wqs, wkn, wv)


def _mla_attn_kernel(qn_ref, qp_ref, kn_ref, kp_ref, v_ref, o_ref, m_ref, acc_ref, *, t, n, scale):
    tile = lambda i: slice(i * t, (i + 1) * t)
    ones = jnp.ones((t, LANES), BF16)

    def score(i, j):
        q = jnp.concatenate([qn_ref[tile(i), :], qp_ref[tile(i), :]], axis=1)
        k = jnp.concatenate([kn_ref[tile(j), :], kp_ref[tile(j), :]], axis=1)
        return _dot_nt(q, k) * scale

    value = lambda j: jnp.concatenate([v_ref[tile(j), :], ones], axis=1)
    _flash_block_causal(n, t, score, value, m_ref, acc_ref)
    _flash_write(n, t, acc_ref, o_ref)


def _mla_attention(qn, qp, kn, kp, v, batch, seq):
    nh, t = MLA_HEADS, min(ATTN_TILE, seq)
    assert seq % t == 0
    n = seq // t
    kern = functools.partial(_mla_attn_kernel, t=t, n=n, scale=(MLA_NOPE + MLA_ROPE) ** -0.5)
    head = pl.BlockSpec((seq, LANES), lambda b, h: (b, h))
    return pl.pallas_call(
        kern,
        grid=(batch, nh),
        in_specs=[head, head, head, pl.BlockSpec((seq, LANES), lambda b, h: (b, 0)), head],
        out_specs=head,
        out_shape=jax.ShapeDtypeStruct((batch * seq, nh * MLA_V), BF16),
        scratch_shapes=[pltpu.VMEM((n, t, 1), F32), pltpu.VMEM((n, t, 2 * LANES), F32)],
        compiler_params=_cparams("parallel", "parallel"),
        name="mla_attention",
    )(qn, qp, kn, kp, v)


def _pad_cols(w, width):
    return jnp.pad(w, ((0, 0), (0, width - w.shape[1])))


def _swap_halves(w):
    half = w.shape[-1] // 2
    return jnp.concatenate([w[..., half:], w[..., :half]], axis=-1)


def _layer_params(i, w_in, gdn_a_log, gdn_dt_bias, mla_w_uq, mla_w_ukv):
    moba_w = MOBA_HEADS * HEAD_DIM
    gk, gv = GDN_HEADS * GDN_DK, GDN_HEADS * GDN_DV
    sizes = (moba_w, moba_w, moba_w, gk, gk, gv, GDN_HEADS, GDN_HEADS, gv, MLA_Q_RANK, MLA_KV_RANK, MLA_ROPE)
    off = np.concatenate([[0], np.cumsum(sizes)]).tolist()
    seg = lambda a: w_in[i, :, off[a]:off[a + 1]]
    w_z = seg(8)
    hb = GDN_HEADS_PER_STEP
    gate_blocks = []
    for g in range(GDN_HEADS // hb):
        gate_blocks.append(_pad_cols(jnp.concatenate(
            [seg(6)[:, g * hb:(g + 1) * hb], seg(7)[:, g * hb:(g + 1) * hb]], axis=1), LANES))
    w_lat = jnp.concatenate([seg(10), _pad_cols(seg(11), LANES), _pad_cols(_swap_halves(seg(11)), LANES), seg(9)]
                            + gate_blocks, axis=1)
    w_lat = _pad_cols(w_lat, -(-w_lat.shape[1] // (4 * LANES)) * (4 * LANES))
    alog_tab = jnp.pad(gdn_a_log[i].reshape(GDN_HEADS // hb, 1, hb), ((0, 0), (0, 0), (0, LANES - hb)))
    dtb_tab = jnp.pad(gdn_dt_bias[i].reshape(GDN_HEADS // hb, 1, hb), ((0, 0), (0, 0), (0, LANES - hb)))
    nh = MLA_HEADS
    wq = mla_w_uq[i].reshape(MLA_Q_RANK, nh, MLA_NOPE + MLA_ROPE)
    wqn = wq[:, :, :MLA_NOPE].reshape(MLA_Q_RANK, nh * MLA_NOPE).astype(BF16)
    pe = wq[:, :, MLA_NOPE:]
    padpe = lambda t: jnp.pad(t, ((0, 0), (0, 0), (0, LANES - MLA_ROPE))).reshape(MLA_Q_RANK, nh * LANES).astype(BF16)
    wkv = mla_w_ukv[i].reshape(MLA_KV_RANK, nh, MLA_NOPE + MLA_V)
    wkn = wkv[:, :, :MLA_NOPE].reshape(MLA_KV_RANK, nh * MLA_NOPE).astype(BF16)
    wv = wkv[:, :, MLA_NOPE:].reshape(MLA_KV_RANK, nh * MLA_V).astype(BF16)
    return dict(w_z=w_z, w_lat=w_lat, alog_tab=alog_tab, dtb_tab=dtb_tab,
                wqn=wqn, wqp=padpe(pe), wqs=padpe(_swap_halves(pe)), wkn=wkn, wv=wv)


def _rope_tables():
    half = MLA_ROPE // 2
    inv_freq = 1.0 / (ROPE_THETA ** (jnp.arange(half, dtype=F32) / half))
    freq = jnp.concatenate([inv_freq, inv_freq, jnp.zeros((LANES - MLA_ROPE,), F32)]).reshape(1, LANES)
    sign = jnp.concatenate([-jnp.ones((half,), F32), jnp.ones((half,), F32),
                            jnp.zeros((LANES - MLA_ROPE,), F32)]).reshape(1, LANES)
    return freq, sign


def kernel(x, p, positions, norm_mix_in, w_in, gdn_conv_w, gdn_a_log, gdn_dt_bias, gdn_norm_w, mla_q_norm_w, mla_w_uq, mla_kv_norm_w, mla_w_ukv, w_branch_gate, w_branch_a, w_branch_b, w_branch_c, w_out, norm_mix_out, norm_ffn_in, w_ffn_gate, w_ffn_up, w_ffn_down, norm_ffn_out, w_ple_gate, w_ple_proj):
    batch, seq, d = x.shape
    m = batch * seq
    depth = w_in.shape[0]
    moba_w = MOBA_HEADS * HEAD_DIM
    gdn_qkv_w = 2 * GDN_HEADS * GDN_DK + GDN_HEADS * GDN_DV
    wide = 3 * moba_w + gdn_qkv_w
    gcol0 = MLA_KV_RANK + 2 * LANES + MLA_Q_RANK
    slopes = 2.0 ** (-8.0 * jnp.arange(1, MOBA_HEADS + 1, dtype=F32) / MOBA_HEADS)
    pos_col = positions.reshape(batch, seq, 1)
    pos_row = positions.reshape(batch, 1, seq)
    pos_col2 = positions.reshape(m, 1)
    freq, sign = _rope_tables()

    xf = x.reshape(m, d)
    h = _rmsnorm(xf, norm_mix_in[0])
    for i in range(depth):
        lp = _layer_params(i, w_in, gdn_a_log, gdn_dt_bias, mla_w_uq, mla_w_ukv)
        big = _matmul(h, w_in[i, :, :wide].astype(BF16), 0, wide, BF16, *PROJ_TILE, "in_proj_wide")
        zg = _matmul(h, lp["w_z"], 0, lp["w_z"].shape[1], BF16, *PROJ_TILE, "in_proj_gate")
        lat = _matmul(h, lp["w_lat"], 0, lp["w_lat"].shape[1], F32, *PROJ_TILE, "in_proj_latent")
        y_a = _moba(big, pos_col, pos_row, slopes, batch, seq)
        y_b = _gdn(big, 3 * moba_w, gdn_conv_w[i], zg, 0, lat, gcol0, lp["alog_tab"], lp["dtb_tab"], gdn_norm_w[i],
                   batch, seq)
        qn, qp, kn, kp, v = _mla_prep(lat, pos_col2, freq, sign, mla_q_norm_w[i], mla_kv_norm_w[i],
                                      lp["wqn"], lp["wqp"], lp["wqs"], lp["wkn"], lp["wv"])
        y_c = _mla_attention(qn, qp, kn, kp, v, batch, seq)
        merged = _gated_merge(h, w_branch_gate, y_a, y_b, y_c, w_branch_a, w_branch_b, w_branch_c, i)
        mixed = _matmul(merged, w_out, i, d, BF16, *PROJ_TILE, "out_proj")
        xf, h = _residual_rmsnorm(xf, mixed, norm_mix_out[i], norm_ffn_in[i], True)
        act = _ffn_up(h, w_ffn_gate, w_ffn_up, i)
        f = _matmul(act, w_ffn_down, i, d, BF16, *DOWN_TILE, "ffn_down", single_buffer=True)
        xf, xb = _residual_rmsnorm(xf, f, norm_ffn_out[i], norm_ffn_out[i], False)
        xf = _ple(xb, xf, p[i].reshape(m, -1).astype(BF16), w_ple_gate, w_ple_proj, i)
        if i + 1 < depth:
            h = _rmsnorm(xf, norm_mix_in[i + 1])
    return xf.reshape(batch, seq, d)
```
